```python
import math
import jax
import jax.numpy as jnp
from jax import lax
import numpy as np

D_MODEL = 1024
BATCH = 1
SEQ = 16384
DEPTH = 2
DEC_BATCH = 32
DEC_SEQ = 1
PAST_LEN = 16384
PAGE_SIZE = 128

RET_HEADS = 4
FOX_HEADS = 4
AB_HEAD_DIM = D_MODEL // 8
AB_WIDTH = RET_HEADS * AB_HEAD_DIM
ROPE_BASE = 10000.0
FORGET_BIAS_INIT = 4.0
NSA_HEADS = 16
NSA_KV_HEADS = 2
NSA_GROUP = NSA_HEADS // NSA_KV_HEADS
NSA_HEAD_DIM = D_MODEL // NSA_HEADS
NSA_WIDTH = NSA_HEADS * NSA_HEAD_DIM
NSA_KV_WIDTH = NSA_KV_HEADS * NSA_HEAD_DIM
CMP_BLOCK = 64
N_SELECT = 16
WINDOW = 512
SELECT_FORCE = 1e4
N_MEM = 256
MEM_HEADS = 4
MEM_HEAD_DIM = D_MODEL // MEM_HEADS
N_GROUPS = 4
EXP_PER_GROUP = 8
N_EXPERTS = N_GROUPS * EXP_PER_GROUP
MOE_TOP_K = 2
D_EXPERT = D_MODEL // 4
MOE_BLOCK = 128
Q_BLOCK = 128
ALPHA = (2 * DEPTH) ** 0.25
BETA = (8 * DEPTH) ** -0.25
LN_EPS = 1e-5
NEG_INF = -1e30

kernel_name = 'hybrid_retention_fox_nsa_hmoe_step'


def layer_norm(x, g, b):
    xf = x.astype(jnp.float32)
    mu = xf.mean(-1, keepdims=True)
    var = jnp.square(xf - mu).mean(-1, keepdims=True)
    return ((xf - mu) * lax.rsqrt(var + LN_EPS) * g + b).astype(x.dtype)


def masked_softmax(logits, mask):
    logits = jnp.where(mask, logits.astype(jnp.float32), NEG_INF)
    m = logits.max(-1, keepdims=True)
    p = jnp.where(mask, jnp.exp(logits - m), 0.0)
    return p / jnp.maximum(p.sum(-1, keepdims=True), 1e-30)


def split_heads(a, n_heads):
    return a.reshape(*a.shape[:-1], n_heads, a.shape[-1] // n_heads)


def to_blocks(a):
    b, l = a.shape[:2]
    return jnp.moveaxis(a.reshape(b, l // Q_BLOCK, Q_BLOCK, *a.shape[2:]), 1, 0)


def from_blocks(a):
    nb, b, qb = a.shape[:3]
    return jnp.moveaxis(a, 0, 1).reshape(b, nb * qb, *a.shape[3:])


def gather_pages(pool, page_table):
    n_seq, n_pages = page_table.shape
    return pool[page_table].reshape(n_seq, n_pages * pool.shape[1], *pool.shape[2:])


def rotary(x, pos):
    half = x.shape[-1] // 2
    freqs = jnp.exp(-math.log(ROPE_BASE) * jnp.arange(half, dtype=jnp.float32) / half)
    ang = pos.astype(jnp.float32)[:, None] * freqs
    cos = jnp.cos(ang)[None, :, None, :]
    sin = jnp.sin(ang)[None, :, None, :]
    x1 = x[..., :half].astype(jnp.float32)
    x2 = x[..., half:].astype(jnp.float32)
    return jnp.concatenate([x1 * cos - x2 * sin, x1 * sin + x2 * cos], -1).astype(x.dtype)


def head_norm(o):
    mu = o.mean(-1, keepdims=True)
    var = jnp.square(o - mu).mean(-1, keepdims=True)
    return (o - mu) * lax.rsqrt(var + LN_EPS)


def ret_log_decay():
    return jnp.log1p(-jnp.exp2(-5.0 - jnp.arange(RET_HEADS, dtype=jnp.float32)))


def retention_chunk(S, q, k, v):
    S = S.astype(jnp.float32)
    q, k, v = q.astype(jnp.float32), k.astype(jnp.float32), v.astype(jnp.float32)
    l = q.shape[1]
    lg = ret_log_decay()
    i = jnp.arange(l, dtype=jnp.float32)
    diff = i[:, None] - i[None, :]
    dmat = jnp.where(diff >= 0, jnp.exp(jnp.maximum(diff, 0.0)[None] * lg[:, None, None]), 0.0)
    inner = jnp.einsum('blhd,bmhd->bhlm', q, k) * dmat[None]
    cross = jnp.exp((i + 1.0)[:, None] * lg[None, :])
    o = (jnp.einsum('bhlm,bmhe->blhe', inner, v)
         + jnp.einsum('blhd,bhde->blhe', q * cross[None, :, :, None], S))
    kdec = jnp.exp((l - 1.0 - i)[:, None] * lg[None, :])
    S = (jnp.exp(l * lg)[None, :, None, None] * S
         + jnp.einsum('blhd,blhe->bhde', k * kdec[None, :, :, None], v))
    return o, S


def fox_attend(q, k, v, c_q, c_k, q_pos, k_pos):
    s = jnp.einsum('blhd,bthd->bhlt', q, k).astype(jnp.float32) * AB_HEAD_DIM ** -0.5
    s = s + jnp.swapaxes(c_q, 1, 2)[..., :, None] - jnp.swapaxes(c_k, 1, 2)[..., None, :]
    mask = (k_pos[None, :] <= q_pos[:, None])[None, None]
    p = masked_softmax(s, mask)
    return jnp.einsum('bhlt,bthd->blhd', p.astype(v.dtype), v)


def ab_project(x, pos, w_in, b_f):
    h = x @ w_in
    W = AB_WIDTH
    rq, rk, rv, rg, fq, fk, fv = (h[..., i * W:(i + 1) * W] for i in range(7))
    rq = rotary(split_heads(rq, RET_HEADS), pos)
    rk = rotary(split_heads(rk, RET_HEADS), pos) * AB_HEAD_DIM ** -0.5
    logf = jax.nn.log_sigmoid((h[..., 7 * W:] + b_f).astype(jnp.float32))
    return (rq, rk, split_heads(rv, RET_HEADS), rg, split_heads(fq, FOX_HEADS),
            split_heads(fk, FOX_HEADS), split_heads(fv, FOX_HEADS), logf)


def ab_output(o_ret, rg, o_fox, w_out):
    b, l = rg.shape[:2]
    o_ret = head_norm(o_ret).reshape(b, l, AB_WIDTH).astype(rg.dtype) * jax.nn.silu(rg)
    return jnp.concatenate([o_ret, o_fox.reshape(b, l, AB_WIDTH).astype(rg.dtype)], -1) @ w_out


def mixer_ab_prompt(x, w_in, b_f, w_out):
    b, l = x.shape[:2]
    pos = jnp.arange(l)
    rq, rk, rv, rg, fq, fk, fv, logf = ab_project(x, pos, w_in, b_f)
    s0 = jnp.zeros((b, RET_HEADS, AB_HEAD_DIM, AB_HEAD_DIM), jnp.float32)

    def ret_step(S, qkv):
        o, S = retention_chunk(S, *qkv)
        return S, o

    s_fin, o_ret = lax.scan(ret_step, s0, (to_blocks(rq), to_blocks(rk), to_blocks(rv)))
    o_ret = from_blocks(o_ret)
    c = jnp.cumsum(logf, axis=1)

    def fox_block(args):
        qb, cb, start = args
        return fox_attend(qb, fk, fv, cb, c, start + jnp.arange(Q_BLOCK), pos)

    starts = jnp.arange(l // Q_BLOCK) * Q_BLOCK
    o_fox = from_blocks(lax.map(fox_block, (to_blocks(fq), to_blocks(c), starts)))
    return ab_output(o_ret, rg, o_fox, w_out), s_fin, fk, fv, logf


def mixer_ab_sample(x, state_ret, cache_fox_k, cache_fox_v, cache_fox_logf, page_table, w_in, b_f, w_out):
    l = x.shape[1]
    pos = PAST_LEN + jnp.arange(l)
    rq, rk, rv, rg, fq, fk, fv, logf = ab_project(x, pos, w_in, b_f)
    o_ret, s_new = retention_chunk(state_ret, rq, rk, rv)
    k_all = jnp.concatenate([gather_pages(cache_fox_k, page_table), fk], 1)
    v_all = jnp.concatenate([gather_pages(cache_fox_v, page_table), fv], 1)
    lf_all = jnp.concatenate([gather_pages(cache_fox_logf, page_table).astype(jnp.float32), logf], 1)
    c_all = jnp.cumsum(lf_all, axis=1)
    o_fox = fox_attend(fq, k_all, v_all, c_all[:, PAST_LEN:], c_all, pos, jnp.arange(PAST_LEN + l))
    return ab_output(o_ret, rg, o_fox, w_out), s_new, fk, fv, logf


def nsa_project(x, w_in):
    h = x @ w_in
    b, l = h.shape[:2]
    q = split_heads(h[..., :NSA_WIDTH], NSA_HEADS)
    kv = h[..., NSA_WIDTH:NSA_WIDTH + 6 * NSA_KV_WIDTH].reshape(b, l, 6, NSA_KV_HEADS, NSA_HEAD_DIM)
    gates = jax.nn.sigmoid(h[..., NSA_WIDTH + 6 * NSA_KV_WIDTH:].astype(jnp.float32)).reshape(b, l, NSA_HEADS, 3)
    return q, gates, tuple(kv[:, :, i] for i in range(6))


def compress_blocks(k, a, w):
    b, t, g, d = k.shape
    kb = k.reshape(b, t // CMP_BLOCK, CMP_BLOCK, g, d)
    return jnp.einsum('bnihd,id->bnhd', kb, a) @ w


def nsa_attend(q, gates, kc, vc, fetch, n_sel, kw, vw, kw_pos, q_pos):
    b, l = q.shape[:2]
    scale = NSA_HEAD_DIM ** -0.5
    qg = q.reshape(b, l, NSA_KV_HEADS, NSA_GROUP, NSA_HEAD_DIM)
    nbc = kc.shape[1]
    c_end = (jnp.arange(nbc) + 1) * CMP_BLOCK - 1
    c_mask = (c_end[None, :] <= q_pos[:, None])[None, :, None, None, :]
    p_cmp = masked_softmax(jnp.einsum('blgrd,bngd->blgrn', qg, kc).astype(jnp.float32) * scale, c_mask)
    o_cmp = jnp.einsum('blgrn,bngd->blgrd', p_cmp.astype(vc.dtype), vc).astype(jnp.float32)
    imp = jnp.pad(p_cmp.sum(3), ((0, 0), (0, 0), (0, 0), (0, n_sel - nbc)))
    blk = jnp.arange(n_sel)[None, :]
    cur = (q_pos // CMP_BLOCK)[:, None]
    forced = ((blk == 0) | (blk == cur) | (blk == cur - 1))[None, :, None, :]
    future = (blk * CMP_BLOCK > q_pos[:, None])[None, :, None, :]
    score = jnp.where(forced, SELECT_FORCE, jnp.where(future, -1.0, imp))
    idx = lax.top_k(score, N_SELECT)[1]
    ks, vs = fetch(idx)
    sel_pos = idx[..., None] * CMP_BLOCK + jnp.arange(CMP_BLOCK)
    n_keys = N_SELECT * CMP_BLOCK
    s_mask = (sel_pos <= q_pos[None, :, None, None, None]).reshape(b, l, NSA_KV_HEADS, 1, n_keys)
    s_sel = jnp.einsum('blgrd,blgnid->blgrni', qg, ks).reshape(b, l, NSA_KV_HEADS, NSA_GROUP, n_keys)
    p_sel = masked_softmax(s_sel.astype(jnp.float32) * scale, s_mask)
    o_sel = jnp.einsum('blgrm,blgmd->blgrd', p_sel.astype(vs.dtype),
                       vs.reshape(b, l, NSA_KV_HEADS, n_keys, NSA_HEAD_DIM)).astype(jnp.float32)
    rel = q_pos[:, None] - kw_pos[None, :]
    w_mask = ((rel >= 0) & (rel <= WINDOW) & (kw_pos[None, :] >= 0))[None, :, None, None, :]
    p_win = masked_softmax(jnp.einsum('blgrd,bkgd->blgrk', qg, kw).astype(jnp.float32) * scale, w_mask)
    o_win = jnp.einsum('blgrk,bkgd->blgrd', p_win.astype(vw.dtype), vw).astype(jnp.float32)
    g = gates.reshape(b, l, NSA_KV_HEADS, NSA_GROUP, 3).astype(jnp.float32)
    o = g[..., 0:1] * o_cmp + g[..., 1:2] * o_sel + g[..., 2:3] * o_win
    return o.reshape(b, l, NSA_WIDTH).astype(q.dtype)


def mixer_c_prompt(x, w_in, a_ck, a_cv, w_ck, w_cv, w_out):
    b, l = x.shape[:2]
    q, gates, (kc, vc, ks, vs, kw, vw) = nsa_project(x, w_in)
    kc_c = compress_blocks(kc, a_ck, w_ck)
    vc_c = compress_blocks(vc, a_cv, w_cv)
    nbs = l // CMP_BLOCK
    n_sel = max(nbs, N_SELECT)
    kb = ks.reshape(b, nbs, CMP_BLOCK, NSA_KV_HEADS, NSA_HEAD_DIM)
    vb = vs.reshape(b, nbs, CMP_BLOCK, NSA_KV_HEADS, NSA_HEAD_DIM)
    bi = jnp.arange(b)[:, None, None, None]
    gi = jnp.arange(NSA_KV_HEADS)[None, None, :, None]

    def fetch(idx):
        ic = jnp.minimum(idx, nbs - 1)
        return kb[bi, ic, :, gi], vb[bi, ic, :, gi]

    pad = ((0, 0), (WINDOW, 0), (0, 0), (0, 0))
    kw_pad = jnp.pad(kw, pad)
    vw_pad = jnp.pad(vw, pad)

    def block(args):
        qb, gb, start = args
        q_pos = start + jnp.arange(Q_BLOCK)
        kwb = lax.dynamic_slice_in_dim(kw_pad, start, WINDOW + Q_BLOCK, axis=1)
        vwb = lax.dynamic_slice_in_dim(vw_pad, start, WINDOW + Q_BLOCK, axis=1)
        kw_pos = start - WINDOW + jnp.arange(WINDOW + Q_BLOCK)
        return nsa_attend(qb, gb, kc_c, vc_c, fetch, n_sel, kwb, vwb, kw_pos, q_pos)

    starts = jnp.arange(l // Q_BLOCK) * Q_BLOCK
    o = from_blocks(lax.map(block, (to_blocks(q), to_blocks(gates), starts)))
    win_buf = min(WINDOW, PAST_LEN)
    lpad = ((0, 0), (max(0, win_buf - l), 0), (0, 0), (0, 0))
    win_k = jnp.pad(kw, lpad)[:, -win_buf:]
    win_v = jnp.pad(vw, lpad)[:, -win_buf:]
    return o @ w_out, (kc, vc, ks, vs, win_k, win_v)


def mixer_c_sample(x, cache_cmp_k, cache_cmp_v, cache_slc_k, cache_slc_v, cache_win_k, cache_win_v,
                   page_table, w_in, a_ck, a_cv, w_ck, w_cv, w_out):
    b, l = x.shape[:2]
    q, gates, (kc, vc, ks, vs, kw, vw) = nsa_project(x, w_in)
    q_pos = PAST_LEN + jnp.arange(l)
    nbc = (PAST_LEN + l) // CMP_BLOCK
    kc_all = jnp.concatenate([gather_pages(cache_cmp_k, page_table), kc], 1)[:, :nbc * CMP_BLOCK]
    vc_all = jnp.concatenate([gather_pages(cache_cmp_v, page_table), vc], 1)[:, :nbc * CMP_BLOCK]
    kc_c = compress_blocks(kc_all, a_ck, w_ck)
    vc_c = compress_blocks(vc_all, a_cv, w_cv)
    npb = PAST_LEN // CMP_BLOCK
    nnb = -(-l // CMP_BLOCK)
    n_sel = max(npb + nnb, N_SELECT)
    bpp = PAGE_SIZE // CMP_BLOCK
    n_pages = page_table.shape[1]
    pool_k = cache_slc_k.reshape(cache_slc_k.shape[0], bpp, CMP_BLOCK, NSA_KV_HEADS, NSA_HEAD_DIM)
    pool_v = cache_slc_v.reshape(cache_slc_v.shape[0], bpp, CMP_BLOCK, NSA_KV_HEADS, NSA_HEAD_DIM)
    pad_new = ((0, 0), (0, nnb * CMP_BLOCK - l), (0, 0), (0, 0))
    new_k = jnp.pad(ks, pad_new).reshape(b, nnb, CMP_BLOCK, NSA_KV_HEADS, NSA_HEAD_DIM)
    new_v = jnp.pad(vs, pad_new).reshape(b, nnb, CMP_BLOCK, NSA_KV_HEADS, NSA_HEAD_DIM)
    bi = jnp.arange(b)[:, None, None, None]
    gi = jnp.arange(NSA_KV_HEADS)[None, None, :, None]

    def fetch(idx):
        page = page_table[bi, jnp.minimum(idx // bpp, n_pages - 1)]
        sub = idx % bpp
        inew = jnp.clip(idx - npb, 0, nnb - 1)
        past = (idx < npb)[..., None, None]
        return (jnp.where(past, pool_k[page, sub, :, gi], new_k[bi, inew, :, gi]),
                jnp.where(past, pool_v[page, sub, :, gi], new_v[bi, inew, :, gi]))

    win_buf = cache_win_k.shape[1]
    kw_all = jnp.concatenate([cache_win_k, kw], 1)
    vw_all = jnp.concatenate([cache_win_v, vw], 1)
    kw_pos = PAST_LEN - win_buf + jnp.arange(win_buf + l)
    o = nsa_attend(q, gates, kc_c, vc_c, fetch, n_sel, kw_all, vw_all, kw_pos, q_pos)
    return o @ w_out, (kc, vc, ks, vs, kw_all[:, -win_buf:], vw_all[:, -win_buf:])


def mem_kv(mem, w_k, w_v):
    return split_heads(mem @ w_k, MEM_HEADS), split_heads(mem @ w_v, MEM_HEADS)


def mem_attend(x, mk, mv, w_q, w_o):
    b, l = x.shape[:2]
    q = split_heads(x @ w_q, MEM_HEADS)
    s = jnp.einsum('blhd,bmhd->bhlm', q, mk).astype(jnp.float32) * MEM_HEAD_DIM ** -0.5
    p = jax.nn.softmax(s, axis=-1)
    return jnp.einsum('bhlm,bmhd->blhd', p.astype(mv.dtype), mv).reshape(b, l, D_MODEL) @ w_o


def hier_moe(x, w_rg, b_rg, w_re, b_re, w1, w3, w2):
    b, l, d = x.shape
    xt = x.reshape(-1, d)
    t = xt.shape[0]
    g_logits = (xt @ w_rg).astype(jnp.float32) + b_rg
    g_sel = jnp.argmax(g_logits, -1)
    g_w = jnp.take_along_axis(jax.nn.softmax(g_logits, -1), g_sel[:, None], -1)[:, 0]
    e_logits = ((xt @ w_re).astype(jnp.float32) + b_re).reshape(t, N_GROUPS, EXP_PER_GROUP)
    e_in = jnp.take_along_axis(e_logits, g_sel[:, None, None], 1)[:, 0]
    top_v, top_i = lax.top_k(e_in, MOE_TOP_K)
    e_w = jax.nn.softmax(top_v, -1) * g_w[:, None]
    e_id = g_sel[:, None] * EXP_PER_GROUP + top_i
    n = t * MOE_TOP_K
    flat_e = e_id.reshape(-1)
    flat_w = e_w.reshape(-1)
    flat_t = jnp.repeat(jnp.arange(t), MOE_TOP_K)
    order = jnp.argsort(flat_e)
    se = flat_e[order]
    counts = jnp.zeros(N_EXPERTS, jnp.int32).at[flat_e].add(1)
    starts = jnp.cumsum(counts) - counts
    pcounts = (counts + MOE_BLOCK - 1) // MOE_BLOCK * MOE_BLOCK
    pends = jnp.cumsum(pcounts)
    pstarts = pends - pcounts
    dest = pstarts[se] + jnp.arange(n) - starts[se]
    n_rows = (-(-n // MOE_BLOCK) + N_EXPERTS) * MOE_BLOCK
    buf_t = jnp.zeros(n_rows, jnp.int32).at[dest].set(flat_t[order])
    buf_w = jnp.zeros(n_rows, jnp.float32).at[dest].set(flat_w[order])
    n_blk = n_rows // MOE_BLOCK
    blk_e = jnp.minimum(jnp.searchsorted(pends, jnp.arange(n_blk) * MOE_BLOCK, side='right'), N_EXPERTS - 1)
    xin = xt[buf_t].reshape(n_blk, MOE_BLOCK, d)

    def expert_block(args):
        xb, e = args
        return (jax.nn.silu(xb @ w1[e]) * (xb @ w3[e])) @ w2[e]

    yb = lax.map(expert_block, (xin, blk_e)).reshape(n_rows, d)
    y = jax.ops.segment_sum(yb * buf_w[:, None].astype(yb.dtype), buf_t, num_segments=t)
    return y.reshape(b, l, d)


def setup_inputs(seed: int = 0) -> dict:
    key = jax.random.key(seed)
    keys = iter(jax.random.split(key, 64))

    def nrm(shape, scale=1.0):
        return jax.random.normal(next(keys), shape, jnp.float32) * scale

    n_pages = PAST_LEN // PAGE_SIZE
    n_used = DEC_BATCH * n_pages
    n_phys = n_used + -(-n_used // 4)
    page_table = jax.random.permutation(next(keys), n_phys)[:n_used].reshape(DEC_BATCH, n_pages).astype(jnp.int32)
    win_buf = min(WINDOW, PAST_LEN)
    fan = D_MODEL ** -0.5
    kv_nsa = (n_phys, PAGE_SIZE, NSA_KV_HEADS, NSA_HEAD_DIM)
    return {
        'x_prompt': nrm((BATCH, SEQ, D_MODEL)),
        'x_sample': nrm((DEC_BATCH, DEC_SEQ, D_MODEL)),
        'state_ret': nrm((DEC_BATCH, RET_HEADS, AB_HEAD_DIM, AB_HEAD_DIM), 0.5),
        'cache_fox_k': nrm((n_phys, PAGE_SIZE, FOX_HEADS, AB_HEAD_DIM)),
        'cache_fox_v': nrm((n_phys, PAGE_SIZE, FOX_HEADS, AB_HEAD_DIM)),
        'cache_fox_logf': jax.nn.log_sigmoid(FORGET_BIAS_INIT + nrm((n_phys, PAGE_SIZE, FOX_HEADS))),
        'cache_cmp_k': nrm(kv_nsa),
        'cache_cmp_v': nrm(kv_nsa),
        'cache_slc_k': nrm(kv_nsa),
        'cache_slc_v': nrm(kv_nsa),
        'cache_win_k': nrm((DEC_BATCH, win_buf, NSA_KV_HEADS, NSA_HEAD_DIM)),
        'cache_win_v': nrm((DEC_BATCH, win_buf, NSA_KV_HEADS, NSA_HEAD_DIM)),
        'cache_mem_k': nrm((DEPTH, DEC_BATCH, N_MEM, MEM_HEADS, MEM_HEAD_DIM)),
        'cache_mem_v': nrm((DEPTH, DEC_BATCH, N_MEM, MEM_HEADS, MEM_HEAD_DIM)),
        'page_table': page_table,
        'mem_prompt': nrm((BATCH, N_MEM, D_MODEL)),
        'w_in0': nrm((D_MODEL, 7 * AB_WIDTH + FOX_HEADS), fan),
        'b_fox_f': FORGET_BIAS_INIT + nrm((FOX_HEADS,), 0.1),
        'w_out0': nrm((2 * AB_WIDTH, D_MODEL), (2 * AB_WIDTH) ** -0.5 * BETA),
        'w_in1': nrm((D_MODEL, NSA_WIDTH + 6 * NSA_KV_WIDTH + 3 * NSA_HEADS), fan),
        'a_ck': (1.0 + nrm((CMP_BLOCK, NSA_HEAD_DIM), 0.1)) / CMP_BLOCK,
        'a_cv': (1.0 + nrm((CMP_BLOCK, NSA_HEAD_DIM), 0.1)) / CMP_BLOCK,
        'w_ck': nrm((NSA_HEAD_DIM, NSA_HEAD_DIM), NSA_HEAD_DIM ** -0.5),
        'w_cv': nrm((NSA_HEAD_DIM, NSA_HEAD_DIM), NSA_HEAD_DIM ** -0.5),
        'w_out1': nrm((NSA_WIDTH, D_MODEL), NSA_WIDTH ** -0.5 * BETA),
        'w_xq': nrm((DEPTH, D_MODEL, D_MODEL), fan),
        'w_xk': nrm((DEPTH, D_MODEL, D_MODEL), fan),
        'w_xv': nrm((DEPTH, D_MODEL, D_MODEL), fan),
        'w_xo': nrm((DEPTH, D_MODEL, D_MODEL), fan * BETA),
        'ln_m_g': 1.0 + nrm((DEPTH, D_MODEL), 0.02),
        'ln_m_b': nrm((DEPTH, D_MODEL), 0.02),
        'ln_x_g': 1.0 + nrm((DEPTH, D_MODEL), 0.02),
        'ln_x_b': nrm((DEPTH, D_MODEL), 0.02),
        'ln_f_g': 1.0 + nrm((DEPTH, D_MODEL), 0.02),
        'ln_f_b': nrm((DEPTH, D_MODEL), 0.02),
        'w_rg': nrm((DEPTH, D_MODEL, N_GROUPS), fan),
        'b_rg': nrm((DEPTH, N_GROUPS), 0.01),
        'w_re': nrm((DEPTH, D_MODEL, N_EXPERTS), fan),
        'b_re': nrm((DEPTH, N_EXPERTS), 0.01),
        'w_e1': nrm((DEPTH, N_EXPERTS, D_MODEL, D_EXPERT), fan),
        'w_e3': nrm((DEPTH, N_EXPERTS, D_MODEL, D_EXPERT), fan),
        'w_e2': nrm((DEPTH, N_EXPERTS, D_EXPERT, D_MODEL), D_EXPERT ** -0.5 * BETA),
    }


def reference(x_prompt, x_sample, state_ret, cache_fox_k, cache_fox_v, cache_fox_logf,
              cache_cmp_k, cache_cmp_v, cache_slc_k, cache_slc_v, cache_win_k, cache_win_v,
              cache_mem_k, cache_mem_v, page_table, mem_prompt,
              w_in0, b_fox_f, w_out0, w_in1, a_ck, a_cv, w_ck, w_cv, w_out1,
              w_xq, w_xk, w_xv, w_xo, ln_m_g, ln_m_b, ln_x_g, ln_x_b, ln_f_g, ln_f_b,
              w_rg, b_rg, w_re, b_re, w_e1, w_e3, w_e2):
    y_p, y_s = x_prompt, x_sample
    mem_k_list, mem_v_list = [], []
    for layer in range(DEPTH):
        if layer % 2 == 0:
            mix_p, ret_p, fk_p, fv_p, ff_p = mixer_ab_prompt(y_p, w_in0, b_fox_f, w_out0)
            mix_s, ret_s, fk_s, fv_s, ff_s = mixer_ab_sample(
                y_s, state_ret, cache_fox_k, cache_fox_v, cache_fox_logf, page_table, w_in0, b_fox_f, w_out0)
        else:
            mix_p, nsa_p = mixer_c_prompt(y_p, w_in1, a_ck, a_cv, w_ck, w_cv, w_out1)
            mix_s, nsa_s = mixer_c_sample(
                y_s, cache_cmp_k, cache_cmp_v, cache_slc_k, cache_slc_v, cache_win_k, cache_win_v,
                page_table, w_in1, a_ck, a_cv, w_ck, w_cv, w_out1)
        y_p = layer_norm(ALPHA * y_p + mix_p, ln_m_g[layer], ln_m_b[layer])
        y_s = layer_norm(ALPHA * y_s + mix_s, ln_m_g[layer], ln_m_b[layer])
        mk, mv = mem_kv(mem_prompt, w_xk[layer], w_xv[layer])
        mem_k_list.append(mk)
        mem_v_list.append(mv)
        y_p = layer_norm(ALPHA * y_p + mem_attend(y_p, mk, mv, w_xq[layer], w_xo[layer]),
                         ln_x_g[layer], ln_x_b[layer])
        y_s = layer_norm(ALPHA * y_s + mem_attend(y_s, cache_mem_k[layer], cache_mem_v[layer], w_xq[layer], w_xo[layer]),
                         ln_x_g[layer], ln_x_b[layer])
        y_p = layer_norm(ALPHA * y_p + hier_moe(y_p, w_rg[layer], b_rg[layer], w_re[layer], b_re[layer],
                                                 w_e1[layer], w_e3[layer], w_e2[layer]), ln_f_g[layer], ln_f_b[layer])
        y_s = layer_norm(ALPHA * y_s + hier_moe(y_s, w_rg[layer], b_rg[layer], w_re[layer], b_re[layer],
                                                 w_e1[layer], w_e3[layer], w_e2[layer]), ln_f_g[layer], ln_f_b[layer])
    cmpk_p, cmpv_p, slck_p, slcv_p, wink_p, winv_p = nsa_p
    cmpk_s, cmpv_s, slck_s, slcv_s, wink_s, winv_s = nsa_s
    mem_k_p = jnp.stack(mem_k_list)
    mem_v_p = jnp.stack(mem_v_list)
    return (y_p, y_s, ret_p, ret_s, fk_p, fv_p, ff_p, fk_s, fv_s, ff_s,
            cmpk_p, cmpv_p, slck_p, slcv_p, wink_p, winv_p,
            cmpk_s, cmpv_s, slck_s, slcv_s, wink_s, winv_s,
            mem_k_p, mem_v_p)
```

```python
import functools
import math

import jax
import jax.numpy as jnp
from jax import lax
from jax.experimental import pallas as pl
from jax.experimental.pallas import tpu as pltpu

F32 = jnp.float32
BF16 = jnp.bfloat16

RET_HEADS = 4
FOX_HEADS = 4
AB_HEAD_DIM = 128
AB_WIDTH = RET_HEADS * AB_HEAD_DIM
ROPE_BASE = 10000.0
NSA_HEADS = 16
NSA_KV_HEADS = 2
NSA_GROUP = NSA_HEADS // NSA_KV_HEADS
NSA_HEAD_DIM = 64
NSA_WIDTH = NSA_HEADS * NSA_HEAD_DIM
NSA_KV_WIDTH = NSA_KV_HEADS * NSA_HEAD_DIM
CMP_BLOCK = 64
N_SELECT = 16
WINDOW = 512
SELECT_FORCE = 1e4
MEM_HEADS = 4
N_GROUPS = 4
EXP_PER_GROUP = 8
N_EXPERTS = N_GROUPS * EXP_PER_GROUP
PAGE_SIZE = 128
RET_CHUNK = 128
DEPTH = 2
ALPHA = (2 * DEPTH) ** 0.25
LN_EPS = 1e-5
NEG_INF = -1e30

LANE = 128
DEC_ROWS = 16
VMEM_LIMIT = 48 * 1024 * 1024


def _cparams(*sem):
    return pltpu.CompilerParams(dimension_semantics=sem, vmem_limit_bytes=VMEM_LIMIT)


def _bf(x):
    return x.astype(BF16)


def _dot(a, b):
    return jnp.dot(a, b, preferred_element_type=F32)


def _dot_nt(a, b):
    return lax.dot_general(a, b, (((1,), (1,)), ((), ())), preferred_element_type=F32)


def _dot_hi(a, b):
    return jnp.dot(a, b, precision=lax.Precision.HIGHEST, preferred_element_type=F32)


def _ln(xf, g, b):
    mu = jnp.mean(xf, -1, keepdims=True)
    d = xf - mu
    var = jnp.mean(d * d, -1, keepdims=True)
    return d * lax.rsqrt(var + LN_EPS) * g + b


def _iota(shape, dim):
    return lax.broadcasted_iota(jnp.int32, shape, dim)


def _mm_kernel(x_ref, w_ref, o_ref):
    o_ref[...] = _dot(_bf(x_ref[...]), w_ref[...])


def _matmul(x, w_bf, tm, tn, name):
    m, k = x.shape
    n = w_bf.shape[1]
    tm = min(tm, m)
    tn = min(tn, n)
    return pl.pallas_call(
        _mm_kernel, grid=(m // tm, n // tn),
        in_specs=[pl.BlockSpec((tm, k), lambda i, j: (i, 0)),
                  pl.BlockSpec((k, tn), lambda i, j: (0, j))],
        out_specs=pl.BlockSpec((tm, tn), lambda i, j: (i, j)),
        out_shape=jax.ShapeDtypeStruct((m, n), F32),
        compiler_params=_cparams("parallel", "arbitrary"), name=name)(x, w_bf)


def _gate_kernel(x_ref, w_ref, b_ref, *rest, act):
    z = _dot(_bf(x_ref[...]), w_ref[...]) + b_ref[...]
    if act == "sigmoid":
        rest[0][...] = 1.0 / (1.0 + jnp.exp(-z))
        return
    lf = jnp.minimum(z, 0.0) - jnp.log1p(jnp.exp(-jnp.abs(z)))
    rest[0][...] = lf
    if act == "logsig_cum":
        c_ref, carry = rest[1], rest[2]

        @pl.when(pl.program_id(0) == 0)
        def _():
            carry[...] = jnp.zeros_like(carry)

        tm = lf.shape[0]
        tri = (_iota((tm, tm), 1) <= _iota((tm, tm), 0)).astype(F32)
        cs = _dot_hi(tri, lf) + carry[...]
        c_ref[...] = cs
        carry[...] = cs[tm - 1:tm, :]


def _gate(x, w_pad_bf, b_pad, act, name):
    m, k = x.shape
    tm = min(256, m)
    row = pl.BlockSpec((tm, LANE), lambda i: (i, 0))
    n_out = 2 if act == "logsig_cum" else 1
    out = pl.pallas_call(
        functools.partial(_gate_kernel, act=act), grid=(m // tm,),
        in_specs=[pl.BlockSpec((tm, k), lambda i: (i, 0)),
                  pl.BlockSpec((k, LANE), lambda i: (0, 0)),
                  pl.BlockSpec((1, LANE), lambda i: (0, 0))],
        out_specs=[row] * n_out,
        out_shape=[jax.ShapeDtypeStruct((m, LANE), F32)] * n_out,
        scratch_shapes=[pltpu.VMEM((1, LANE), F32)] if act == "logsig_cum" else [],
        compiler_params=_cparams("arbitrary"), name=name)(x, w_pad_bf, b_pad)
    return out


def _pad_cols(w, width=LANE):
    return jnp.pad(w, ((0, 0), (0, width - w.shape[1])))


def _rope_tables(pos):
    half = AB_HEAD_DIM // 2
    freqs = jnp.exp(-math.log(ROPE_BASE) * jnp.arange(half, dtype=F32) / half)
    ang = pos.astype(F32)[:, None] * freqs
    cos, sin = jnp.cos(ang), jnp.sin(ang)
    return jnp.concatenate([cos, cos], -1), jnp.concatenate([-sin, sin], -1)


def _ret_log_decay():
    return jnp.log1p(-jnp.exp2(-5.0 - jnp.arange(RET_HEADS, dtype=F32)))


def _rot(x, cosf, sins):
    return x * cosf + pltpu.roll(x, AB_HEAD_DIM // 2, 1) * sins


def _head_norm_gate(o, g):
    mu = jnp.mean(o, -1, keepdims=True)
    d = o - mu
    var = jnp.mean(d * d, -1, keepdims=True)
    return d * lax.rsqrt(var + LN_EPS) * (g * (1.0 / (1.0 + jnp.exp(-g))))


def _ret_kernel(q_ref, k_ref, v_ref, g_ref, cos_ref, sin_ref, dmat_ref, cross_ref, kdec_ref,
                sdec_ref, o_ref, s_ref, state):
    @pl.when(pl.program_id(0) == 0)
    def _():
        state[...] = jnp.zeros_like(state)

    cosf, sins = cos_ref[...], sin_ref[...]
    for h in range(RET_HEADS):
        sl = slice(h * AB_HEAD_DIM, (h + 1) * AB_HEAD_DIM)
        qh = _rot(q_ref[:, sl], cosf, sins)
        kh = _rot(k_ref[:, sl], cosf, sins) * AB_HEAD_DIM ** -0.5
        vh = _bf(v_ref[:, sl])
        sh = state[h]
        inner = _dot_nt(_bf(qh), _bf(kh)) * dmat_ref[h]
        o = _dot(_bf(inner), vh) + _dot(_bf(qh * cross_ref[:, sl]), _bf(sh))
        kd = kh * kdec_ref[:, sl]
        state[h] = sdec_ref[h] * sh + _dot(_bf(kd.T), vh)
        o_ref[:, sl] = _head_norm_gate(o, g_ref[:, sl])
    s_ref[...] = state[...]


def _retention_prompt(h, t):
    lc = RET_CHUNK
    pos = jnp.arange(t)
    cosf, sins = _rope_tables(pos)
    lg = _ret_log_decay()
    i = jnp.arange(lc, dtype=F32)
    diff = i[:, None] - i[None, :]
    dmat = jnp.where(diff >= 0, jnp.exp(jnp.maximum(diff, 0.0)[None] * lg[:, None, None]), 0.0)
    cross = jnp.repeat(jnp.exp((i + 1.0)[:, None] * lg[None, :]), AB_HEAD_DIM, axis=1)
    kdec = jnp.repeat(jnp.exp((lc - 1.0 - i)[:, None] * lg[None, :]), AB_HEAD_DIM, axis=1)
    sdec = jnp.broadcast_to(jnp.exp(lc * lg)[:, None, None], (RET_HEADS, AB_HEAD_DIM, AB_HEAD_DIM))
    col = lambda c: pl.BlockSpec((lc, AB_WIDTH), lambda n, c=c: (n, c))
    tab = pl.BlockSpec((lc, AB_HEAD_DIM), lambda n: (n, 0))
    const2 = pl.BlockSpec((lc, AB_WIDTH), lambda n: (0, 0))
    const3 = pl.BlockSpec((RET_HEADS, AB_HEAD_DIM, AB_HEAD_DIM), lambda n: (0, 0, 0))
    return pl.pallas_call(
        _ret_kernel, grid=(t // lc,),
        in_specs=[col(0), col(1), col(2), col(3), tab, tab, const3, const2, const2, const3],
        out_specs=[pl.BlockSpec((lc, AB_WIDTH), lambda n: (n, 0)), const3],
        out_shape=[jax.ShapeDtypeStruct((t, AB_WIDTH), F32),
                   jax.ShapeDtypeStruct((RET_HEADS, AB_HEAD_DIM, AB_HEAD_DIM), F32)],
        scratch_shapes=[pltpu.VMEM((RET_HEADS, AB_HEAD_DIM, AB_HEAD_DIM), F32)],
        compiler_params=_cparams("arbitrary"), name="retention_prompt")(
            h, h, h, h, cosf, sins, dmat, cross, kdec, sdec)


def _ret_step_kernel(q_ref, k_ref, v_ref, g_ref, cos_ref, sin_ref, gam_ref, s_ref, o_ref, so_ref):
    cosf, sins = cos_ref[...], sin_ref[...]
    d = AB_HEAD_DIM
    eye = _iota((d, d), 0) == _iota((d, d), 1)
    for h in range(RET_HEADS):
        sl = slice(h * d, (h + 1) * d)
        qh = _rot(q_ref[0][:, sl], cosf, sins)
        kh = _rot(k_ref[0][:, sl], cosf, sins) * d ** -0.5
        vh = v_ref[0][:, sl]
        gam = gam_ref[h:h + 1, :]
        qcol = jnp.sum(jnp.where(eye, jnp.broadcast_to(qh, (d, d)), 0.0), axis=1, keepdims=True)
        kcol = jnp.sum(jnp.where(eye, jnp.broadcast_to(kh, (d, d)), 0.0), axis=1, keepdims=True)
        sh = s_ref[0, h]
        qk = jnp.sum(qh * kh, axis=1, keepdims=True)
        o = qk * vh + jnp.sum((qcol * gam) * sh, axis=0, keepdims=True)
        so_ref[0, h] = gam * sh + kcol * vh
        o_ref[0, :, sl] = _head_norm_gate(o, g_ref[0][:, sl])


def _retention_step(h_s, state, pos0):
    b = h_s.shape[0]
    cosf, sins = _rope_tables(jnp.full((1,), pos0))
    gam = jnp.broadcast_to(jnp.exp(_ret_log_decay())[:, None], (RET_HEADS, AB_HEAD_DIM))
    h3 = h_s.reshape(b, 1, h_s.shape[1])
    col = lambda c: pl.BlockSpec((1, 1, AB_WIDTH), lambda n, c=c: (n, 0, c))
    tab = pl.BlockSpec((1, AB_HEAD_DIM), lambda n: (0, 0))
    sspec = pl.BlockSpec((1, RET_HEADS, AB_HEAD_DIM, AB_HEAD_DIM), lambda n: (n, 0, 0, 0))
    o, s_new = pl.pallas_call(
        _ret_step_kernel, grid=(b,),
        in_specs=[col(0), col(1), col(2), col(3), tab, tab,
                  pl.BlockSpec((RET_HEADS, AB_HEAD_DIM), lambda n: (0, 0)), sspec],
        out_specs=[pl.BlockSpec((1, 1, AB_WIDTH), lambda n: (n, 0, 0)), sspec],
        out_shape=[jax.ShapeDtypeStruct((b, 1, AB_WIDTH), F32),
                   jax.ShapeDtypeStruct(state.shape, F32)],
        compiler_params=_cparams("parallel"), name="retention_step")(
            h3, h3, h3, h3, cosf, sins, gam, state)
    return o.reshape(b, AB_WIDTH), s_new


def _fox_kernel(q_ref, k_ref, v_ref, cq_ref, ck_ref, o_ref, m_sc, l_sc, acc_sc, *, tq, tk):
    i, j = pl.program_id(0), pl.program_id(1)
    d = AB_HEAD_DIM

    @pl.when(j == 0)
    def _():
        m_sc[...] = jnp.full_like(m_sc, NEG_INF)
        l_sc[...] = jnp.zeros_like(l_sc)
        acc_sc[...] = jnp.zeros_like(acc_sc)

    @pl.when(j <= i)
    def _():
        qpos = i * tq + _iota((tq, tk), 0)
        kpos = j * tk + _iota((tq, tk), 1)
        mask = kpos <= qpos
        for h in range(FOX_HEADS):
            sl = slice(h * d, (h + 1) * d)
            s = _dot_nt(_bf(q_ref[:, sl]), _bf(k_ref[:, sl])) * d ** -0.5
            s = s + cq_ref[:, h:h + 1] - ck_ref[h:h + 1, :]
            s = jnp.where(mask, s, NEG_INF)
            m_old = m_sc[h]
            m_new = jnp.maximum(m_old, jnp.max(s, -1, keepdims=True))
            p = jnp.where(mask, jnp.exp(s - m_new), 0.0)
            a = jnp.exp(m_old - m_new)
            l_sc[h] = a * l_sc[h] + jnp.sum(p, -1, keepdims=True)
            acc_sc[:, sl] = a * acc_sc[:, sl] + _dot(_bf(p), _bf(v_ref[:, sl]))
            m_sc[h] = m_new

    @pl.when(j == i)
    def _():
        for h in range(FOX_HEADS):
            sl = slice(h * d, (h + 1) * d)
            o_ref[:, sl] = acc_sc[:, sl] / jnp.maximum(l_sc[h], 1e-30)


def _fox_prompt(h, c, c_t, t):
    tq = tk = min(512, t)
    n = t // tq
    return pl.pallas_call(
        functools.partial(_fox_kernel, tq=tq, tk=tk), grid=(n, n),
        in_specs=[pl.BlockSpec((tq, AB_WIDTH), lambda i, j: (i, 4)),
                  pl.BlockSpec((tk, AB_WIDTH), lambda i, j: (jnp.minimum(i, j), 5)),
                  pl.BlockSpec((tk, AB_WIDTH), lambda i, j: (jnp.minimum(i, j), 6)),
                  pl.BlockSpec((tq, LANE), lambda i, j: (i, 0)),
                  pl.BlockSpec((8, tk), lambda i, j: (0, jnp.minimum(i, j)))],
        out_specs=pl.BlockSpec((tq, AB_WIDTH), lambda i, j: (i, 0)),
        out_shape=jax.ShapeDtypeStruct((t, AB_WIDTH), F32),
        scratch_shapes=[pltpu.VMEM((FOX_HEADS, tq, 1), F32), pltpu.VMEM((FOX_HEADS, tq, 1), F32),
                        pltpu.VMEM((tq, AB_WIDTH), F32)],
        compiler_params=_cparams("parallel", "arbitrary"), name="fox_flash")(h, h, h, c, c_t)


def _proj_ln_kernel(*refs, sizes):
    pos, acc = 0, None
    for n in sizes:
        a = refs[pos][...]
        for t in range(1, n):
            a = a + refs[pos + t][...]
        d = _dot(_bf(a), refs[pos + n][...])
        acc = d if acc is None else acc + d
        pos += n + 1
    x_ref, g_ref, b_ref, o_ref = refs[pos:pos + 4]
    o_ref[...] = _ln(ALPHA * x_ref[...] + acc, g_ref[...], b_ref[...])


def _proj_ln(groups, x, g, b, name):
    m, dm = x.shape
    tm = min(512, m)
    args, specs, sizes = [], [], []
    for arrs, w in groups:
        sizes.append(len(arrs))
        for a in arrs:
            args.append(a)
            specs.append(pl.BlockSpec((tm, a.shape[1]), lambda i: (i, 0)))
        args.append(w)
        specs.append(pl.BlockSpec(w.shape, lambda i: (0, 0)))
    vec = pl.BlockSpec((1, dm), lambda i: (0, 0))
    row = pl.BlockSpec((tm, dm), lambda i: (i, 0))
    return pl.pallas_call(
        functools.partial(_proj_ln_kernel, sizes=tuple(sizes)), grid=(m // tm,),
        in_specs=specs + [row, vec, vec], out_specs=row,
        out_shape=jax.ShapeDtypeStruct((m, dm), F32),
        compiler_params=_cparams("parallel"), name=name)(*args, x, g.reshape(1, dm), b.reshape(1, dm))


def _mem_kernel(x_ref, wq_ref, mk_ref, mv_ref, wo_ref, g_ref, b_ref, o_ref):
    x = x_ref[...]
    q = _dot(_bf(x), wq_ref[...])
    dh = q.shape[1] // MEM_HEADS
    outs = []
    for h in range(MEM_HEADS):
        sl = slice(h * dh, (h + 1) * dh)
        s = _dot_nt(_bf(q[:, sl]), mk_ref[:, sl]) * dh ** -0.5
        e = jnp.exp(s - jnp.max(s, -1, keepdims=True))
        p = e / jnp.sum(e, -1, keepdims=True)
        outs.append(_dot(_bf(p), mv_ref[:, sl]))
    y = _dot(_bf(jnp.concatenate(outs, -1)), wo_ref[...])
    o_ref[...] = _ln(ALPHA * x + y, g_ref[...], b_ref[...])


def _mem_prompt(x, wq, mk_bf, mv_bf, wo, g, b):
    m, dm = x.shape
    tm = min(256, m)
    full = lambda a: pl.BlockSpec(a.shape, lambda i: (0, 0))
    vec = pl.BlockSpec((1, dm), lambda i: (0, 0))
    row = pl.BlockSpec((tm, dm), lambda i: (i, 0))
    return pl.pallas_call(
        _mem_kernel, grid=(m // tm,),
        in_specs=[row, full(wq), full(mk_bf), full(mv_bf), full(wo), vec, vec], out_specs=row,
        out_shape=jax.ShapeDtypeStruct((m, dm), F32),
        compiler_params=_cparams("parallel"), name="mem_attn_prompt")(
            x, wq, mk_bf, mv_bf, wo, g.reshape(1, dm), b.reshape(1, dm))


def _router_kernel(x_ref, w_ref, b_ref, pk_ref, cnt_ref, carry):
    @pl.when(pl.program_id(0) == 0)
    def _():
        carry[...] = jnp.zeros_like(carry)

    z = _dot_hi(x_ref[...], w_ref[...]) + b_ref[...]
    tm = z.shape[0]
    lane = _iota((tm, LANE), 1)
    lanef = lane.astype(F32)
    big = float(LANE)

    def first_max(v):
        mx = jnp.max(v, -1, keepdims=True)
        return mx, jnp.min(jnp.where(v == mx, lanef, big), -1, keepdims=True)

    gmask = lane < N_GROUPS
    gmax, gsel = first_max(jnp.where(gmask, z, -jnp.inf))
    g_w = 1.0 / jnp.sum(jnp.where(gmask, jnp.exp(z - gmax), 0.0), -1, keepdims=True)
    lo = N_GROUPS + EXP_PER_GROUP * gsel
    ze = jnp.where((lanef >= lo) & (lanef < lo + EXP_PER_GROUP), z, -jnp.inf)
    v1, i1 = first_max(ze)
    v2, i2 = first_max(jnp.where(lanef == i1, -jnp.inf, ze))
    e2 = jnp.exp(v2 - v1)
    w1 = (1.0 / (1.0 + e2)) * g_w
    w2 = (e2 / (1.0 + e2)) * g_w
    id1, id2 = i1 - N_GROUPS, i2 - N_GROUPS
    oh1 = (lanef == id1).astype(F32)
    oh2 = (lanef == id2).astype(F32)
    both = oh1 + oh2
    tri = (_iota((tm, tm), 1) < _iota((tm, tm), 0)).astype(BF16)
    cnt = _dot(tri, _bf(both)) + carry[...]
    r1 = jnp.sum(cnt * oh1, -1, keepdims=True)
    r2 = jnp.sum(cnt * oh2, -1, keepdims=True) + jnp.sum(oh1 * oh2, -1, keepdims=True)
    carry[...] = carry[...] + jnp.sum(both, axis=0, keepdims=True)
    cnt_ref[...] = jnp.broadcast_to(carry[...], cnt_ref.shape)
    pk = jnp.zeros((tm, LANE), F32)
    for k, v in enumerate((id1, id2, w1, w2, r1, r2)):
        pk = jnp.where(lane == k, v, pk)
    pk_ref[...] = pk


def _router(x, w_r, b_r):
    m, dm = x.shape
    tm = min(256, m)
    return pl.pallas_call(
        _router_kernel, grid=(m // tm,),
        in_specs=[pl.BlockSpec((tm, dm), lambda i: (i, 0)),
                  pl.BlockSpec((dm, LANE), lambda i: (0, 0)),
                  pl.BlockSpec((1, LANE), lambda i: (0, 0))],
        out_specs=[pl.BlockSpec((tm, LANE), lambda i: (i, 0)),
                   pl.BlockSpec((8, LANE), lambda i: (0, 0))],
        out_shape=[jax.ShapeDtypeStruct((m, LANE), F32), jax.ShapeDtypeStruct((8, LANE), F32)],
        scratch_shapes=[pltpu.VMEM((1, LANE), F32)],
        compiler_params=_cparams("arbitrary"), name="moe_router")(x, w_r, b_r)


def _row_copy(src, s, dst, d, sem):
    return pltpu.make_async_copy(src.at[pl.ds(s, 1), :], dst.at[pl.ds(d, 1), :], sem)


def _dispatch_kernel(dest_ref, x_ref, zin_ref, xin_ref, sem, *, tm):
    del zin_ref
    base = pl.program_id(0) * tm * 2

    def issue(r, c):
        _row_copy(x_ref, r, xin_ref, dest_ref[base + 2 * r], sem).start()
        _row_copy(x_ref, r, xin_ref, dest_ref[base + 2 * r + 1], sem).start()
        return c

    def drain(r, c):
        _row_copy(x_ref, 0, xin_ref, 0, sem).wait()
        _row_copy(x_ref, 0, xin_ref, 0, sem).wait()
        return c

    lax.fori_loop(0, tm, issue, 0)
    lax.fori_loop(0, tm, drain, 0)


def _dispatch(x, dest_flat, n_rows):
    m, dm = x.shape
    tm = min(256, m)
    return pl.pallas_call(
        functools.partial(_dispatch_kernel, tm=tm),
        grid_spec=pltpu.PrefetchScalarGridSpec(
            num_scalar_prefetch=1, grid=(m // tm,),
            in_specs=[pl.BlockSpec((tm, dm), lambda i, d: (i, 0)),
                      pl.BlockSpec(memory_space=pl.ANY)],
            out_specs=pl.BlockSpec(memory_space=pl.ANY),
            scratch_shapes=[pltpu.SemaphoreType.DMA(())]),
        out_shape=jax.ShapeDtypeStruct((n_rows, dm), F32),
        input_output_aliases={2: 0},
        compiler_params=_cparams("arbitrary"), name="moe_dispatch")(
            dest_flat, x, jnp.zeros((n_rows, dm), F32))


def _expert_kernel(be_ref, nu_ref, x_ref, w1_ref, w3_ref, w2_ref, o_ref, w1b, w3b, w2b):
    i = pl.program_id(0)
    changed = (i == 0) | (be_ref[i] != be_ref[jnp.maximum(i - 1, 0)])

    @pl.when(changed)
    def _():
        w1b[...] = _bf(w1_ref[0, 0])
        w3b[...] = _bf(w3_ref[0, 0])
        w2b[...] = _bf(w2_ref[0, 0])

    @pl.when(i < nu_ref[0])
    def _():
        x = _bf(x_ref[...])
        a = _dot(x, w1b[...])
        hid = (a * (1.0 / (1.0 + jnp.exp(-a)))) * _dot(x, w3b[...])
        o_ref[...] = _dot(_bf(hid), w2b[...])

    @pl.when(i >= nu_ref[0])
    def _():
        o_ref[...] = jnp.zeros_like(o_ref)


def _experts(xin, blk_e, n_used, w1, w3, w2, layer, blk):
    n_rows, dm = xin.shape
    de = w1.shape[-1]
    wspec = lambda shp: pl.BlockSpec((1, 1) + shp, lambda i, be, nu: (layer, be[i], 0, 0))
    return pl.pallas_call(
        _expert_kernel,
        grid_spec=pltpu.PrefetchScalarGridSpec(
            num_scalar_prefetch=2, grid=(n_rows // blk,),
            in_specs=[pl.BlockSpec((blk, dm), lambda i, be, nu: (i, 0)),
                      wspec((dm, de)), wspec((dm, de)), wspec((de, dm))],
            out_specs=pl.BlockSpec((blk, dm), lambda i, be, nu: (i, 0)),
            scratch_shapes=[pltpu.VMEM((dm, de), BF16), pltpu.VMEM((dm, de), BF16),
                            pltpu.VMEM((de, dm), BF16)]),
        out_shape=jax.ShapeDtypeStruct((n_rows, dm), F32),
        compiler_params=_cparams("arbitrary"), name="moe_experts")(blk_e, n_used, xin, w1, w3, w2)


def _combine_kernel(dest_ref, yb_ref, pk_ref, x_ref, g_ref, b_ref, o_ref, buf0, buf1, sem, *, tm):
    base = pl.program_id(0) * tm * 2

    def issue(r, c):
        _row_copy(yb_ref, dest_ref[base + 2 * r], buf0, r, sem).start()
        _row_copy(yb_ref, dest_ref[base + 2 * r + 1], buf1, r, sem).start()
        return c

    def drain(r, c):
        _row_copy(yb_ref, 0, buf0, 0, sem).wait()
        _row_copy(yb_ref, 0, buf1, 0, sem).wait()
        return c

    lax.fori_loop(0, tm, issue, 0)
    lax.fori_loop(0, tm, drain, 0)
    pk = pk_ref[...]
    y = buf0[...] * pk[:, 2:3] + buf1[...] * pk[:, 3:4]
    o_ref[...] = _ln(ALPHA * x_ref[...] + y, g_ref[...], b_ref[...])


def _combine(yb, dest_flat, pk, x, g, b):
    m, dm = x.shape
    tm = min(128, m)
    vec = pl.BlockSpec((1, dm), lambda i, d: (0, 0))
    row = pl.BlockSpec((tm, dm), lambda i, d: (i, 0))
    return pl.pallas_call(
        functools.partial(_combine_kernel, tm=tm),
        grid_spec=pltpu.PrefetchScalarGridSpec(
            num_scalar_prefetch=1, grid=(m // tm,),
            in_specs=[pl.BlockSpec(memory_space=pl.ANY),
                      pl.BlockSpec((tm, LANE), lambda i, d: (i, 0)), row, vec, vec],
            out_specs=row,
            scratch_shapes=[pltpu.VMEM((tm, dm), F32), pltpu.VMEM((tm, dm), F32),
                            pltpu.SemaphoreType.DMA(())]),
        out_shape=jax.ShapeDtypeStruct((m, dm), F32),
        compiler_params=_cparams("arbitrary"), name="moe_combine")(
            dest_flat, yb, pk, x, g.reshape(1, dm), b.reshape(1, dm))


def _hier_moe_ln(x, layer, blk, w_rg, b_rg, w_re, b_re, w_e1, w_e3, w_e2, g, b):
    m, dm = x.shape
    w_r = _pad_cols(jnp.concatenate([w_rg[layer], w_re[layer]], 1))
    b_r = _pad_cols(jnp.concatenate([b_rg[layer], b_re[layer]])[None, :])
    pk, cnt = _router(x, w_r, b_r)
    e_id = pk[:, 0:2].astype(jnp.int32)
    rank = pk[:, 4:6].astype(jnp.int32)
    counts = cnt[0, :N_EXPERTS].astype(jnp.int32)
    pcounts = (counts + blk - 1) // blk * blk
    pends = jnp.cumsum(pcounts)
    pstarts = pends - pcounts
    dest = (pstarts[e_id] + rank).reshape(-1)
    n_rows = (-(-(2 * m) // blk) + N_EXPERTS) * blk
    n_blk = n_rows // blk
    blk_e = jnp.minimum(jnp.searchsorted(pends, jnp.arange(n_blk) * blk, side='right'),
                        N_EXPERTS - 1).astype(jnp.int32)
    n_used = (pends[-1:] // blk).astype(jnp.int32)
    xin = _dispatch(x, dest, n_rows)
    yb = _experts(xin, blk_e, n_used, w_e1, w_e3, w_e2, layer, blk)
    return _combine(yb, dest, pk, x, g, b)


def _compress_kernel(*refs, n_pages, per_page, n_prefetch=0):
    refs = refs[n_prefetch:]
    ak, av, wk, wv, ok, ov = refs[2 * n_pages:]
    for pages, a_ref, w_ref, o_ref in ((refs[:n_pages], ak, wk, ok), (refs[n_pages:2 * n_pages], av, wv, ov)):
        pooled = []
        for p_ref in pages:
            x = p_ref[...].reshape(per_page, CMP_BLOCK, LANE) * a_ref[...][None]
            pooled.append(jnp.sum(x, axis=1))
        pooled = pooled[0] if n_pages == 1 else jnp.concatenate(pooled, 0)
        o_ref[...] = _dot(_bf(pooled), w_ref[...]).reshape(o_ref.shape)


def _compress_consts(a_ck, a_cv, w_ck, w_cv):
    eye = jnp.eye(NSA_KV_HEADS, dtype=F32)
    return (jnp.tile(a_ck, (1, NSA_KV_HEADS)), jnp.tile(a_cv, (1, NSA_KV_HEADS)),
            _bf(jnp.kron(eye, w_ck)), _bf(jnp.kron(eye, w_cv)))


def _compress_prompt(h1, t, consts):
    rows = min(2048, t)
    per = rows // CMP_BLOCK
    c2 = lambda a: pl.BlockSpec(a.shape, lambda i: (0, 0))
    kcol = NSA_WIDTH // LANE
    out = pl.BlockSpec((per, LANE), lambda i: (i, 0))
    return pl.pallas_call(
        functools.partial(_compress_kernel, n_pages=1, per_page=per), grid=(t // rows,),
        in_specs=[pl.BlockSpec((rows, LANE), lambda i: (i, kcol)),
                  pl.BlockSpec((rows, LANE), lambda i: (i, kcol + 1))] + [c2(a) for a in consts],
        out_specs=[out, out],
        out_shape=[jax.ShapeDtypeStruct((t // CMP_BLOCK, LANE), F32)] * 2,
        compiler_params=_cparams("parallel"), name="nsa_compress_prompt")(h1, h1, *consts)


def _compress_paged(pool_k, pool_v, table, consts):
    b, n_pages = table.shape
    pp = min(16, n_pages)
    per = PAGE_SIZE // CMP_BLOCK
    page = lambda p: pl.BlockSpec((None, PAGE_SIZE, LANE), lambda n, j, t, p=p: (t[n, j * pp + p], 0, 0))
    c2 = lambda a: pl.BlockSpec(a.shape, lambda n, j, t: (0, 0))
    out = pl.BlockSpec((1, pp * per, LANE), lambda n, j, t: (n, j, 0))
    nblk = n_pages * per
    return pl.pallas_call(
        functools.partial(_compress_kernel, n_pages=pp, per_page=per, n_prefetch=1),
        grid_spec=pltpu.PrefetchScalarGridSpec(
            num_scalar_prefetch=1, grid=(b, n_pages // pp),
            in_specs=[page(p) for p in range(pp)] * 2 + [c2(a) for a in consts],
            out_specs=[out, out]),
        out_shape=[jax.ShapeDtypeStruct((b, nblk, LANE), F32)] * 2,
        compiler_params=_cparams("parallel", "arbitrary"), name="nsa_compress_paged")(
            table, *([pool_k] * pp), *([pool_v] * pp), *consts)


def _select_mask(score, n_pick):
    n = score.shape[1]
    lanef = _iota(score.shape, 1).astype(F32)

    def body(_, carry):
        sel, sc = carry
        mx = jnp.max(sc, -1, keepdims=True)
        idx = jnp.min(jnp.where(sc == mx, lanef, float(n)), -1, keepdims=True)
        hit = lanef == idx
        return jnp.where(hit, 1.0, sel), jnp.where(hit, -2.0, sc)

    sel, _ = lax.fori_loop(0, n_pick, body, (jnp.zeros_like(score), score))
    return sel


def _stack_heads(q_ref, g):
    d = NSA_HEAD_DIM
    return jnp.concatenate(
        [q_ref[:, (g * NSA_GROUP + r) * d:(g * NSA_GROUP + r + 1) * d] for r in range(NSA_GROUP)], 0)


def _cmp_select_kernel(q_ref, kc_ref, vc_ref, gate_ref, o_ref, sel_ref, *, tq, nbp):
    i = pl.program_id(0)
    d = NSA_HEAD_DIM
    rows = NSA_GROUP * tq
    t_rows = i * tq + (_iota((rows, nbp), 0) & (tq - 1))
    blk_rows = _iota((rows, nbp), 1)
    cmask = (blk_rows + 1) * CMP_BLOCK - 1 <= t_rows
    t = i * tq + _iota((tq, nbp), 0)
    blk = _iota((tq, nbp), 1)
    cur = t >> 6
    forced = (blk == 0) | (blk == cur) | (blk == cur - 1)
    future = blk * CMP_BLOCK > t
    for g in range(NSA_KV_HEADS):
        sl = slice(g * d, (g + 1) * d)
        qs = _bf(_stack_heads(q_ref, g))
        logits = jnp.where(cmask, _dot_nt(qs, _bf(kc_ref[:, sl])) * d ** -0.5, NEG_INF)
        p = jnp.where(cmask, jnp.exp(logits - jnp.max(logits, -1, keepdims=True)), 0.0)
        p = p / jnp.maximum(jnp.sum(p, -1, keepdims=True), 1e-30)
        o = _dot(_bf(p), _bf(vc_ref[:, sl]))
        imp = p[0:tq]
        for r in range(1, NSA_GROUP):
            imp = imp + p[r * tq:(r + 1) * tq]
        score = jnp.where(forced, SELECT_FORCE, jnp.where(future, -1.0, imp))
        sel_ref[:, g * nbp:(g + 1) * nbp] = _select_mask(score, N_SELECT)
        for r in range(NSA_GROUP):
            hh = g * NSA_GROUP + r
            o_ref[:, hh * d:(hh + 1) * d] = o[r * tq:(r + 1) * tq] * gate_ref[:, hh * 3:hh * 3 + 1]


def _cmp_select_prompt(h1, kc_c, vc_c, gates, t, nbp):
    tq = min(128, t)
    c2 = lambda a: pl.BlockSpec(a.shape, lambda i: (0, 0))
    return pl.pallas_call(
        functools.partial(_cmp_select_kernel, tq=tq, nbp=nbp), grid=(t // tq,),
        in_specs=[pl.BlockSpec((tq, NSA_WIDTH), lambda i: (i, 0)), c2(kc_c), c2(vc_c),
                  pl.BlockSpec((tq, LANE), lambda i: (i, 0))],
        out_specs=[pl.BlockSpec((tq, NSA_WIDTH), lambda i: (i, 0)),
                   pl.BlockSpec((tq, NSA_KV_HEADS * nbp), lambda i: (i, 0))],
        out_shape=[jax.ShapeDtypeStruct((t, NSA_WIDTH), F32),
                   jax.ShapeDtypeStruct((t, NSA_KV_HEADS * nbp), F32)],
        compiler_params=_cparams("parallel"), name="nsa_cmp_select_prompt")(h1, kc_c, vc_c, gates)


def _nsa_flash_kernel(*refs, mode, tq, tk, nbp, gate_j):
    if mode == "sel":
        q_ref, k_ref, v_ref, sel_ref, gate_ref, o_ref, qs, m_sc, l_sc, acc_sc = refs
    else:
        q_ref, k_ref, v_ref, gate_ref, o_ref, qs, m_sc, l_sc, acc_sc = refs
    i, j = pl.program_id(0), pl.program_id(1)
    d = NSA_HEAD_DIM
    rows = NSA_GROUP * tq

    @pl.when(j == 0)
    def _():
        m_sc[...] = jnp.full_like(m_sc, NEG_INF)
        l_sc[...] = jnp.zeros_like(l_sc)
        acc_sc[...] = jnp.zeros_like(acc_sc)
        for g in range(NSA_KV_HEADS):
            qs[g] = _bf(_stack_heads(q_ref, g))

    if mode == "sel":
        kt = j
        last = (i * tq + tq - 1) // tk
        valid = j <= last
    else:
        kt = i * tq // tk - WINDOW // tk + j
        last = WINDOW // tk
        valid = kt >= 0

    @pl.when(valid)
    def _():
        t = i * tq + (_iota((rows, tk), 0) & (tq - 1))
        kpos = kt * tk + _iota((rows, tk), 1)
        rel = t - kpos
        pos_ok = (rel >= 0) if mode == "sel" else ((rel >= 0) & (rel <= WINDOW))
        for g in range(NSA_KV_HEADS):
            sl = slice(g * d, (g + 1) * d)
            s = _dot_nt(qs[g], _bf(k_ref[:, sl])) * d ** -0.5
            if mode == "sel":
                per_tile = tk // CMP_BLOCK
                expand = (_iota((nbp, tk), 0) == kt * per_tile + (_iota((nbp, tk), 1) >> 6)).astype(BF16)
                hit = _dot(_bf(sel_ref[:, g * nbp:(g + 1) * nbp]), expand)
                mask = pos_ok & (jnp.concatenate([hit] * NSA_GROUP, 0) > 0.5)
            else:
                mask = pos_ok
            s = jnp.where(mask, s, NEG_INF)
            m_old = m_sc[g]
            m_new = jnp.maximum(m_old, jnp.max(s, -1, keepdims=True))
            p = jnp.where(mask, jnp.exp(s - m_new), 0.0)
            a = jnp.exp(m_old - m_new)
            l_sc[g] = a * l_sc[g] + jnp.sum(p, -1, keepdims=True)
            acc_sc[g] = a * acc_sc[g] + _dot(_bf(p), _bf(v_ref[:, sl]))
            m_sc[g] = m_new

    @pl.when(j == last)
    def _():
        for g in range(NSA_KV_HEADS):
            o = acc_sc[g] / jnp.maximum(l_sc[g], 1e-30)
            for r in range(NSA_GROUP):
                hh = g * NSA_GROUP + r
                o_ref[:, hh * d:(hh + 1) * d] = (
                    o[r * tq:(r + 1) * tq] * gate_ref[:, hh * 3 + gate_j:hh * 3 + gate_j + 1])


def _nsa_flash_prompt(h1, sel, gates, t, nbp, mode):
    tq = min(128, t)
    kcol = NSA_WIDTH // LANE
    qspec = pl.BlockSpec((tq, NSA_WIDTH), lambda i, j: (i, 0))
    gspec = pl.BlockSpec((tq, LANE), lambda i, j: (i, 0))
    if mode == "sel":
        tk = min(512, t)
        nk = t // tk
        kv = lambda c: pl.BlockSpec((tk, LANE), lambda i, j, c=c: (jnp.minimum(j, (i * tq + tq - 1) // tk), c))
        ins = [qspec, kv(kcol + 2), kv(kcol + 3),
               pl.BlockSpec((tq, NSA_KV_HEADS * nbp), lambda i, j: (i, 0)), gspec]
        args = (h1, h1, h1, sel, gates)
        gate_j = 1
    else:
        tk = tq
        nk = WINDOW // tk + 1
        kv = lambda c: pl.BlockSpec((tk, LANE), lambda i, j, c=c: (jnp.maximum(i - WINDOW // tk + j, 0), c))
        ins = [qspec, kv(kcol + 4), kv(kcol + 5), gspec]
        args = (h1, h1, h1, gates)
        gate_j = 2
    rows = NSA_GROUP * tq
    return pl.pallas_call(
        functools.partial(_nsa_flash_kernel, mode=mode, tq=tq, tk=tk, nbp=nbp, gate_j=gate_j),
        grid=(t // tq, nk), in_specs=ins,
        out_specs=pl.BlockSpec((tq, NSA_WIDTH), lambda i, j: (i, 0)),
        out_shape=jax.ShapeDtypeStruct((t, NSA_WIDTH), F32),
        scratch_shapes=[pltpu.VMEM((NSA_KV_HEADS, rows, NSA_HEAD_DIM), BF16),
                        pltpu.VMEM((NSA_KV_HEADS, rows, 1), F32),
                        pltpu.VMEM((NSA_KV_HEADS, rows, 1), F32),
                        pltpu.VMEM((NSA_KV_HEADS, rows, NSA_HEAD_DIM), F32)],
        compiler_params=_cparams("parallel", "arbitrary"), name="nsa_flash_" + mode)(*args)


def _page_cumsum_kernel(x_ref, o_ref):
    n = x_ref.shape[1]
    tri = (_iota((n, n), 0) <= _iota((n, n), 1)).astype(F32)
    o_ref[...] = _dot_hi(x_ref[...], tri)


def _page_cumsum(x):
    m, n = x.shape
    tm = 1024
    while m % tm:
        tm //= 2
    return pl.pallas_call(
        _page_cumsum_kernel, grid=(m // tm,),
        in_specs=[pl.BlockSpec((tm, n), lambda i: (i, 0))],
        out_specs=pl.BlockSpec((tm, n), lambda i: (i, 0)),
        out_shape=jax.ShapeDtypeStruct((m, n), F32),
        compiler_params=_cparams("parallel"), name="fox_page_cumsum")(x)


def _decode_kernel(*refs, pp, scale, bias, masked, new, scaled, nbp):
    table_ref = refs[0]
    del table_ref
    pos = 1
    q_ref = refs[pos]; pos += 1
    k_refs = refs[pos:pos + pp]; pos += pp
    v_refs = refs[pos:pos + pp]; pos += pp
    if bias:
        c_refs = refs[pos:pos + pp]; pos += pp
    if masked:
        sel_ref = refs[pos]; pos += 1
    if new:
        kn_ref, vn_ref = refs[pos:pos + 2]; pos += 2
    if bias:
        bn_ref = refs[pos]; pos += 1
    if scaled:
        rs_ref = refs[pos]; pos += 1
    o_ref = refs[pos]; pos += 1
    m_sc, l_sc, acc_sc, carry = refs[pos:pos + 4]
    j = pl.program_id(1)
    nj = pl.num_programs(1)

    @pl.when(j == 0)
    def _():
        m_sc[...] = jnp.full_like(m_sc, NEG_INF)
        l_sc[...] = jnp.zeros_like(l_sc)
        acc_sc[...] = jnp.zeros_like(acc_sc)
        carry[...] = jnp.zeros_like(carry)

    q = _bf(q_ref[0])
    s = jnp.concatenate([_dot_nt(q, _bf(k[...])) for k in k_refs], 1) * scale
    if bias:
        run = carry[...]
        cs = []
        for c in c_refs:
            cin = c[...]
            cs.append(run + cin)
            run = run + cin[:, PAGE_SIZE - 1:PAGE_SIZE]
        carry[...] = run
        s = s - jnp.concatenate(cs, 1)
    if masked:
        w = pp * PAGE_SIZE
        per = w // CMP_BLOCK
        expand = (_iota((nbp, w), 0) == j * per + (_iota((nbp, w), 1) >> 6)).astype(BF16)
        mask = _dot(_bf(sel_ref[0]), expand) > 0.5
        s = jnp.where(mask, s, NEG_INF)
    m_old = m_sc[...]
    m_new = jnp.maximum(m_old, jnp.max(s, -1, keepdims=True))
    p = jnp.exp(s - m_new)
    if masked:
        p = jnp.where(mask, p, 0.0)
    a = jnp.exp(m_old - m_new)
    l_sc[...] = a * l_sc[...] + jnp.sum(p, -1, keepdims=True)
    pv = None
    for n, v in enumerate(v_refs):
        d = _dot(_bf(p[:, n * PAGE_SIZE:(n + 1) * PAGE_SIZE]), _bf(v[...]))
        pv = d if pv is None else pv + d
    acc_sc[...] = a * acc_sc[...] + pv
    m_sc[...] = m_new

    @pl.when(j == nj - 1)
    def _():
        m1, l1, acc = m_sc[...], l_sc[...], acc_sc[...]
        if new:
            qf = q.astype(F32)
            sn = jnp.sum(qf * _bf(kn_ref[0]).astype(F32), -1, keepdims=True) * scale
            if bias:
                sn = sn - (carry[...] + bn_ref[0][:, 0:1])
            m2 = jnp.maximum(m1, sn)
            a2 = jnp.exp(m1 - m2)
            pn = jnp.exp(sn - m2)
            l1 = a2 * l1 + pn
            acc = a2 * acc + pn * _bf(vn_ref[0]).astype(F32)
        out = acc / jnp.maximum(l1, 1e-30)
        if scaled:
            out = out * rs_ref[0][:, 0:1]
        o_ref[0] = out


def _decode(qrows, kpool, vpool, table, pp, scale, *, cpool=None, sel=None, knew=None, vnew=None,
            bnew=None, rowscale=None, name):
    b, r, dk = qrows.shape
    dv = vpool.shape[-1]
    n_pages = table.shape[1]
    pp = min(pp, n_pages)
    per_b = lambda shp: pl.BlockSpec((1,) + shp, lambda n, j, t: (n, 0, 0))
    page = lambda w, p, rws=PAGE_SIZE: pl.BlockSpec(
        (None, rws, w), lambda n, j, t, p=p: (t[n, j * pp + p], 0, 0))
    args = [qrows] + [kpool] * pp + [vpool] * pp
    specs = [per_b((r, dk))] + [page(dk, p) for p in range(pp)] + [page(dv, p) for p in range(pp)]
    nbp = 0
    if cpool is not None:
        args += [cpool] * pp
        specs += [page(PAGE_SIZE, p, r) for p in range(pp)]
    if sel is not None:
        nbp = sel.shape[-1]
        args.append(sel)
        specs.append(per_b((r, nbp)))
    if knew is not None:
        args += [knew, vnew]
        specs += [per_b((1, dk)), per_b((1, dv))]
    if bnew is not None:
        args.append(bnew)
        specs.append(per_b((r, LANE)))
    if rowscale is not None:
        args.append(rowscale)
        specs.append(per_b((r, LANE)))
    kern = functools.partial(_decode_kernel, pp=pp, scale=scale, bias=cpool is not None,
                             masked=sel is not None, new=knew is not None,
                             scaled=rowscale is not None, nbp=nbp)
    return pl.pallas_call(
        kern,
        grid_spec=pltpu.PrefetchScalarGridSpec(
            num_scalar_prefetch=1, grid=(b, n_pages // pp), in_specs=specs,
            out_specs=per_b((r, dv)),
            scratch_shapes=[pltpu.VMEM((r, 1), F32), pltpu.VMEM((r, 1), F32),
                            pltpu.VMEM((r, dv), F32), pltpu.VMEM((r, 1), F32)]),
        out_shape=jax.ShapeDtypeStruct((b, r, dv), F32),
        compiler_params=_cparams("parallel", "arbitrary"), name=name)(table, *args)


def _contig_table(b, pages_per_seq):
    return (jnp.arange(b, dtype=jnp.int32)[:, None] * pages_per_seq
            + jnp.arange(pages_per_seq, dtype=jnp.int32)[None, :])


def _head_rows(x, n_heads, n_rows):
    b, w = x.shape
    d = w // n_heads
    hm = (jnp.arange(n_rows)[:, None] == (jnp.arange(w) // d)[None, :]).astype(x.dtype)
    return x[:, None, :] * hm[None]


def _head_diag(acc, n_heads):
    d = acc.shape[-1] // n_heads
    return jnp.concatenate([acc[:, h, h * d:(h + 1) * d] for h in range(n_heads)], -1)


def _nsa_rows(q):
    b = q.shape[0]
    qh = q.reshape(b, NSA_KV_HEADS, NSA_GROUP, 1, NSA_HEAD_DIM)
    eye = jnp.eye(NSA_KV_HEADS, dtype=q.dtype)[None, :, None, :, None]
    return (qh * eye).reshape(b, NSA_HEADS, NSA_KV_WIDTH)


def _nsa_diag(acc):
    b = acc.shape[0]
    a = acc.reshape(b, NSA_KV_HEADS, NSA_GROUP, NSA_KV_HEADS, NSA_HEAD_DIM)
    return jnp.concatenate([a[:, g, :, g, :] for g in range(NSA_KV_HEADS)], 1).reshape(b, NSA_WIDTH)


def _gate_rows(gates, j):
    b = gates.shape[0]
    g = gates[:, :NSA_HEADS * 3].reshape(b, NSA_HEADS, 3)[:, :, j]
    return jnp.broadcast_to(g[:, :, None], (b, NSA_HEADS, LANE))


def _cmp_select_step_kernel(q_ref, kc_ref, vc_ref, rs_ref, o_ref, sel_ref, *, nbp, n_vis):
    q = _bf(q_ref[0])
    blk = _iota((NSA_HEADS, nbp), 1)
    cmask = blk < n_vis
    logits = jnp.where(cmask, _dot_nt(q, _bf(kc_ref[0])) * NSA_HEAD_DIM ** -0.5, NEG_INF)
    p = jnp.where(cmask, jnp.exp(logits - jnp.max(logits, -1, keepdims=True)), 0.0)
    p = p / jnp.maximum(jnp.sum(p, -1, keepdims=True), 1e-30)
    o_ref[0] = _dot(_bf(p), _bf(vc_ref[0])) * rs_ref[0][:, 0:1]
    blk1 = _iota((1, nbp), 1)
    forced = (blk1 == 0) | (blk1 == n_vis - 1)
    for g in range(NSA_KV_HEADS):
        imp = jnp.sum(p[g * NSA_GROUP:(g + 1) * NSA_GROUP], axis=0, keepdims=True)
        score = jnp.where(forced, SELECT_FORCE, jnp.where(blk1 >= n_vis, -1.0, imp))
        sel = _select_mask(jnp.broadcast_to(score, (8, nbp)), N_SELECT - 1)[0:1]
        sel_ref[0, g * NSA_GROUP:(g + 1) * NSA_GROUP, :] = jnp.broadcast_to(sel, (NSA_GROUP, nbp))


def _cmp_select_step(qrows, kc_c, vc_c, rowscale, n_vis):
    b = qrows.shape[0]
    nbp = kc_c.shape[1]
    per_b = lambda shp: pl.BlockSpec((1,) + shp, lambda n: (n, 0, 0))
    return pl.pallas_call(
        functools.partial(_cmp_select_step_kernel, nbp=nbp, n_vis=n_vis), grid=(b,),
        in_specs=[per_b((NSA_HEADS, NSA_KV_WIDTH)), per_b((nbp, LANE)), per_b((nbp, LANE)),
                  per_b((NSA_HEADS, LANE))],
        out_specs=[per_b((NSA_HEADS, LANE)), per_b((NSA_HEADS, nbp))],
        out_shape=[jax.ShapeDtypeStruct((b, NSA_HEADS, LANE), F32),
                   jax.ShapeDtypeStruct((b, NSA_HEADS, nbp), F32)],
        compiler_params=_cparams("parallel"), name="nsa_cmp_select_step")(qrows, kc_c, vc_c, rowscale)


def kernel(x_prompt, x_sample, state_ret, cache_fox_k, cache_fox_v, cache_fox_logf, cache_cmp_k, cache_cmp_v, cache_slc_k, cache_slc_v, cache_win_k, cache_win_v, cache_mem_k, cache_mem_v, page_table, mem_prompt, w_in0, b_fox_f, w_out0, w_in1, a_ck, a_cv, w_ck, w_cv, w_out1, w_xq, w_xk, w_xv, w_xo, ln_m_g, ln_m_b, ln_x_g, ln_x_b, ln_f_g, ln_f_b, w_rg, b_rg, w_re, b_re, w_e1, w_e3, w_e2):
    bp, t, dm = x_prompt.shape
    bs, ls, _ = x_sample.shape
    assert bp == 1 and ls == 1
    n_phys = cache_fox_k.shape[0]
    n_pages = page_table.shape[1]
    past = n_pages * PAGE_SIZE
    win_buf = cache_win_k.shape[1]
    assert t % 512 == 0 and t // CMP_BLOCK >= N_SELECT and win_buf == WINDOW and past % CMP_BLOCK == 0
    page_table = page_table.astype(jnp.int32)
    y_p = x_prompt.reshape(t, dm)
    y_s = x_sample.reshape(bs, dm)

    n_mem = mem_prompt.shape[1]
    mem_k_list, mem_v_list = [], []

    def sublayers(y_p, y_s, mix_p_groups, mix_s_groups, layer):
        y_p = _proj_ln(mix_p_groups, y_p, ln_m_g[layer], ln_m_b[layer], "mix_out_ln_prompt")
        y_s = _proj_ln(mix_s_groups, y_s, ln_m_g[layer], ln_m_b[layer], "mix_out_ln_step")
        w_kv = _bf(jnp.concatenate([w_xk[layer], w_xv[layer]], 1))
        mkv = _matmul(mem_prompt.reshape(n_mem, dm), w_kv, n_mem, 512, "mem_kv")
        mk, mv = mkv[:, :dm], mkv[:, dm:]
        mem_k_list.append(mk.reshape(1, n_mem, MEM_HEADS, dm // MEM_HEADS))
        mem_v_list.append(mv.reshape(1, n_mem, MEM_HEADS, dm // MEM_HEADS))
        wq, wo = _bf(w_xq[layer]), _bf(w_xo[layer])
        y_p = _mem_prompt(y_p, wq, _bf(mk), _bf(mv), wo, ln_x_g[layer], ln_x_b[layer])
        q_s = _matmul(y_s, wq, bs, 512, "mem_q_step")
        mpages = n_mem // PAGE_SIZE
        acc = _decode(_head_rows(q_s, MEM_HEADS, DEC_ROWS),
                      cache_mem_k[layer].reshape(bs * mpages, PAGE_SIZE, dm),
                      cache_mem_v[layer].reshape(bs * mpages, PAGE_SIZE, dm),
                      _contig_table(bs, mpages), mpages, (dm // MEM_HEADS) ** -0.5, name="mem_attn_step")
        y_s = _proj_ln([([_head_diag(acc, MEM_HEADS)], wo)], y_s, ln_x_g[layer], ln_x_b[layer],
                       "mem_out_ln_step")
        moe = (w_rg, b_rg, w_re, b_re, w_e1, w_e3, w_e2, ln_f_g[layer], ln_f_b[layer])
        y_p = _hier_moe_ln(y_p, layer, 128, *moe)
        y_s = _hier_moe_ln(y_s, layer, 16, *moe)
        return y_p, y_s

    n_main = 7 * AB_WIDTH
    w_main = _bf(w_in0[:, :n_main])
    w_f = _bf(_pad_cols(w_in0[:, n_main:]))
    b_f = _pad_cols(b_fox_f[None, :])
    w0a, w0b = _bf(w_out0[:AB_WIDTH]), _bf(w_out0[AB_WIDTH:])
    h = _matmul(y_p, w_main, 1024, 512, "ab_proj_prompt")
    logf_p, c_p = _gate(y_p, w_f, b_f, "logsig_cum", "fox_gate_prompt")
    o_ret_p, ret_p = _retention_prompt(h, t)
    c_t = jnp.pad(c_p[:, :FOX_HEADS].T, ((0, 8 - FOX_HEADS), (0, 0)))
    o_fox_p = _fox_prompt(h, c_p, c_t, t)
    fk_p = h[:, 5 * AB_WIDTH:6 * AB_WIDTH].reshape(1, t, FOX_HEADS, AB_HEAD_DIM)
    fv_p = h[:, 6 * AB_WIDTH:7 * AB_WIDTH].reshape(1, t, FOX_HEADS, AB_HEAD_DIM)
    ff_p = logf_p[:, :FOX_HEADS].reshape(1, t, FOX_HEADS)
    h_s = _matmul(y_s, w_main, bs, 512, "ab_proj_step")
    logf_s = _gate(y_s, w_f, b_f, "logsig", "fox_gate_step")[0]
    o_ret_s, ret_s = _retention_step(h_s, state_ret, past)
    lf_t = jnp.pad(jnp.swapaxes(cache_fox_logf.astype(F32), 1, 2), ((0, 0), (0, DEC_ROWS - FOX_HEADS), (0, 0)))
    cpool = _page_cumsum(lf_t.reshape(n_phys * DEC_ROWS, PAGE_SIZE)).reshape(n_phys, DEC_ROWS, PAGE_SIZE)
    fq_s = h_s[:, 4 * AB_WIDTH:5 * AB_WIDTH]
    fk_s = h_s[:, 5 * AB_WIDTH:6 * AB_WIDTH]
    fv_s = h_s[:, 6 * AB_WIDTH:7 * AB_WIDTH]
    bnew = jnp.broadcast_to(
        jnp.pad(logf_s[:, :FOX_HEADS], ((0, 0), (0, DEC_ROWS - FOX_HEADS)))[:, :, None], (bs, DEC_ROWS, LANE))
    acc = _decode(_head_rows(fq_s, FOX_HEADS, DEC_ROWS),
                  cache_fox_k.reshape(n_phys, PAGE_SIZE, AB_WIDTH),
                  cache_fox_v.reshape(n_phys, PAGE_SIZE, AB_WIDTH),
                  page_table, 8, AB_HEAD_DIM ** -0.5, cpool=cpool,
                  knew=fk_s.reshape(bs, 1, AB_WIDTH), vnew=fv_s.reshape(bs, 1, AB_WIDTH), bnew=bnew,
                  name="fox_decode")
    o_fox_s = _head_diag(acc, FOX_HEADS)
    y_p, y_s = sublayers(y_p, y_s,
                         [([o_ret_p], w0a), ([o_fox_p], w0b)],
                         [([o_ret_s], w0a), ([o_fox_s], w0b)], 0)

    n_qkv = NSA_WIDTH + 6 * NSA_KV_WIDTH
    w1_main = _bf(w_in1[:, :n_qkv])
    w1_g = _bf(_pad_cols(w_in1[:, n_qkv:]))
    zero_b = jnp.zeros((1, LANE), F32)
    w_o1 = _bf(w_out1)
    consts = _compress_consts(a_ck, a_cv, w_ck, w_cv)
    kv_split = lambda hh, i: hh[:, NSA_WIDTH + i * NSA_KV_WIDTH:NSA_WIDTH + (i + 1) * NSA_KV_WIDTH]
    h1 = _matmul(y_p, w1_main, 1024, 256, "nsa_proj_prompt")
    gates_p = _gate(y_p, w1_g, zero_b, "sigmoid", "nsa_gate_prompt")[0]
    nb = t // CMP_BLOCK
    nbp = -(-nb // LANE) * LANE
    kc_c, vc_c = _compress_prompt(h1, t, consts)
    if nbp > nb:
        kc_c = jnp.pad(kc_c, ((0, nbp - nb), (0, 0)))
        vc_c = jnp.pad(vc_c, ((0, nbp - nb), (0, 0)))
    o_cmp_p, sel_p = _cmp_select_prompt(h1, kc_c, vc_c, gates_p, t, nbp)
    o_sel_p = _nsa_flash_prompt(h1, sel_p, gates_p, t, nbp, "sel")
    o_win_p = _nsa_flash_prompt(h1, None, gates_p, t, nbp, "win")
    kv4 = lambda a, n: a.reshape(-1, n, NSA_KV_HEADS, NSA_HEAD_DIM)
    nsa_p = [kv4(kv_split(h1, i), t) for i in range(4)]
    nsa_p += [kv4(kv_split(h1, i)[t - win_buf:], win_buf) for i in (4, 5)]
    h1_s = _matmul(y_s, w1_main, bs, 256, "nsa_proj_step")
    gates_s = _gate(y_s, w1_g, zero_b, "sigmoid", "nsa_gate_step")[0]
    qrows = _nsa_rows(h1_s[:, :NSA_WIDTH])
    pool3 = lambda a: a.reshape(a.shape[0], PAGE_SIZE, NSA_KV_WIDTH)
    kc_s, vc_s = _compress_paged(pool3(cache_cmp_k), pool3(cache_cmp_v), page_table, consts)
    nbc = past // CMP_BLOCK
    nbp_s = -(-nbc // LANE) * LANE
    if nbp_s > nbc:
        kc_s = jnp.pad(kc_s, ((0, 0), (0, nbp_s - nbc), (0, 0)))
        vc_s = jnp.pad(vc_s, ((0, 0), (0, nbp_s - nbc), (0, 0)))
    acc_cmp, sel_s = _cmp_select_step(qrows, kc_s, vc_s, _gate_rows(gates_s, 0), nbc)
    scale = NSA_HEAD_DIM ** -0.5
    new_kv = lambda i: kv_split(h1_s, i).reshape(bs, 1, NSA_KV_WIDTH)
    acc_sel = _decode(qrows, pool3(cache_slc_k), pool3(cache_slc_v), page_table, 8, scale,
                      sel=sel_s, knew=new_kv(2), vnew=new_kv(3), rowscale=_gate_rows(gates_s, 1),
                      name="nsa_sel_decode")
    wpages = win_buf // PAGE_SIZE
    acc_win = _decode(qrows, cache_win_k.reshape(bs * wpages, PAGE_SIZE, NSA_KV_WIDTH),
                      cache_win_v.reshape(bs * wpages, PAGE_SIZE, NSA_KV_WIDTH),
                      _contig_table(bs, wpages), wpages, scale,
                      knew=new_kv(4), vnew=new_kv(5), rowscale=_gate_rows(gates_s, 2),
                      name="nsa_win_decode")
    nsa_s = [kv4(kv_split(h1_s, i), 1) for i in range(4)]
    nsa_s += [jnp.concatenate([c, kv4(kv_split(h1_s, i), 1)], 1)[:, -win_buf:]
              for c, i in ((cache_win_k, 4), (cache_win_v, 5))]
    y_p, y_s = sublayers(y_p, y_s,
                         [([o_cmp_p, o_sel_p, o_win_p], w_o1)],
                         [([_nsa_diag(acc_cmp), _nsa_diag(acc_sel), _nsa_diag(acc_win)], w_o1)], 1)

    return (y_p.reshape(1, t, dm), y_s.reshape(bs, 1, dm), ret_p[None], ret_s,
            fk_p, fv_p, ff_p,
            fk_s.reshape(bs, 1, FOX_HEADS, AB_HEAD_DIM), fv_s.reshape(bs, 1, FOX_HEADS, AB_HEAD_DIM),
            logf_s[:, :FOX_HEADS].reshape(bs, 1, FOX_HEADS),
            *nsa_p, *nsa_s,
            jnp.stack(mem_k_list), jnp.stack(mem_v_list))
```

```python
import functools
import math

import jax
import jax.numpy as jnp
import numpy as np
from jax import lax
from jax.experimental import pallas as pl
from jax.experimental.pallas import tpu as pltpu

F32 = jnp.float32
BF16 = jnp.bfloat16

RET_HEADS = 4
FOX_HEADS = 4
AB_HEAD_DIM = 128
AB_WIDTH = RET_HEADS * AB_HEAD_DIM
ROPE_BASE = 10000.0
NSA_HEADS = 16
NSA_KV_HEADS = 2
NSA_GROUP = NSA_HEADS // NSA_KV_HEADS
NSA_HEAD_DIM = 64
NSA_WIDTH = NSA_HEADS * NSA_HEAD_DIM
NSA_KV_WIDTH = NSA_KV_HEADS * NSA_HEAD_DIM
CMP_BLOCK = 64
N_SELECT = 16
WINDOW = 512
SELECT_FORCE = 1e4
MEM_HEADS = 4
N_GROUPS = 4
EXP_PER_GROUP = 8
N_EXPERTS = N_GROUPS * EXP_PER_GROUP
PAGE_SIZE = 128
RET_CHUNK = 128
DEPTH = 2
ALPHA = (2 * DEPTH) ** 0.25
LN_EPS = 1e-5
NEG_INF = -1e30
LOG2E = math.log2(math.e)

LANE = 128
DEC_ROWS = 16
VMEM_LIMIT = 48 * 1024 * 1024


def _cparams(*sem):
    return pltpu.CompilerParams(dimension_semantics=sem, vmem_limit_bytes=VMEM_LIMIT)


def _bf(x):
    return x.astype(BF16)


def _dot(a, b):
    return jnp.dot(a, b, preferred_element_type=F32)


def _dot_nt(a, b):
    return lax.dot_general(a, b, (((1,), (1,)), ((), ())), preferred_element_type=F32)


def _dot_hi(a, b):
    return jnp.dot(a, b, precision=lax.Precision.HIGHEST, preferred_element_type=F32)


def _ln(xf, g, b):
    mu = jnp.mean(xf, -1, keepdims=True)
    d = xf - mu
    var = jnp.mean(d * d, -1, keepdims=True)
    return d * lax.rsqrt(var + LN_EPS) * g + b


def _iota(shape, dim):
    return lax.broadcasted_iota(jnp.int32, shape, dim)


def _mm_kernel(x_ref, w_ref, o_ref):
    o_ref[...] = _dot(_bf(x_ref[...]), w_ref[...])


def _matmul(x, w_bf, tm, tn, name):
    m, k = x.shape
    n = w_bf.shape[1]
    tm = min(tm, m)
    tn = min(tn, n)
    return pl.pallas_call(
        _mm_kernel, grid=(m // tm, n // tn),
        in_specs=[pl.BlockSpec((tm, k), lambda i, j: (i, 0)),
                  pl.BlockSpec((k, tn), lambda i, j: (0, j))],
        out_specs=pl.BlockSpec((tm, tn), lambda i, j: (i, j)),
        out_shape=jax.ShapeDtypeStruct((m, n), F32),
        compiler_params=_cparams("parallel", "arbitrary"), name=name)(x, w_bf)


def _gate_kernel(x_ref, w_ref, b_ref, *rest, act):
    z = _dot(_bf(x_ref[...]), w_ref[...]) + b_ref[...]
    if act == "sigmoid":
        rest[0][...] = 1.0 / (1.0 + jnp.exp(-z))
        return
    lf = jnp.minimum(z, 0.0) - jnp.log1p(jnp.exp(-jnp.abs(z)))
    rest[0][...] = lf
    if act == "logsig_cum":
        c_ref, carry = rest[1], rest[2]

        @pl.when(pl.program_id(0) == 0)
        def _():
            carry[...] = jnp.zeros_like(carry)

        tm = lf.shape[0]
        tri = (_iota((tm, tm), 1) <= _iota((tm, tm), 0)).astype(F32)
        cs = _dot_hi(tri, lf) + carry[...]
        c_ref[...] = cs
        carry[...] = cs[tm - 1:tm, :]


def _gate(x, w_pad_bf, b_pad, act, name):
    m, k = x.shape
    tm = min(256, m)
    row = pl.BlockSpec((tm, LANE), lambda i: (i, 0))
    n_out = 2 if act == "logsig_cum" else 1
    out = pl.pallas_call(
        functools.partial(_gate_kernel, act=act), grid=(m // tm,),
        in_specs=[pl.BlockSpec((tm, k), lambda i: (i, 0)),
                  pl.BlockSpec((k, LANE), lambda i: (0, 0)),
                  pl.BlockSpec((1, LANE), lambda i: (0, 0))],
        out_specs=[row] * n_out,
        out_shape=[jax.ShapeDtypeStruct((m, LANE), F32)] * n_out,
        scratch_shapes=[pltpu.VMEM((1, LANE), F32)] if act == "logsig_cum" else [],
        compiler_params=_cparams("arbitrary"), name=name)(x, w_pad_bf, b_pad)
    return out


def _pad_cols(w, width=LANE):
    return jnp.pad(w, ((0, 0), (0, width - w.shape[1])))


def _rope_tables(pos):
    half = AB_HEAD_DIM // 2
    lane = _iota(pos.shape, 1)
    freqs = jnp.exp(-math.log(ROPE_BASE) * (lane & (half - 1)).astype(F32) / half)
    ang = pos.astype(F32) * freqs
    sin = jnp.sin(ang)
    return jnp.cos(ang), jnp.where(lane < half, -sin, sin)


def _ret_log_decay():
    return jnp.log1p(-jnp.exp2(-5.0 - jnp.arange(RET_HEADS, dtype=F32)))


def _rot(x, cosf, sins):
    return x * cosf + pltpu.roll(x, AB_HEAD_DIM // 2, 1) * sins


def _head_norm_gate(o, g):
    mu = jnp.mean(o, -1, keepdims=True)
    d = o - mu
    var = jnp.mean(d * d, -1, keepdims=True)
    return d * lax.rsqrt(var + LN_EPS) * (g * (1.0 / (1.0 + jnp.exp(-g))))


def _ret_kernel(q_ref, k_ref, v_ref, g_ref, dmat_ref, cross_ref, kdec_ref, sdec_ref, o_ref, s_ref, state):
    @pl.when(pl.program_id(0) == 0)
    def _():
        state[...] = jnp.zeros_like(state)

    lc = q_ref.shape[0]
    cosf, sins = _rope_tables(pl.program_id(0) * lc + _iota((lc, AB_HEAD_DIM), 0))
    for h in range(RET_HEADS):
        sl = slice(h * AB_HEAD_DIM, (h + 1) * AB_HEAD_DIM)
        qh = _rot(q_ref[:, sl], cosf, sins)
        kh = _rot(k_ref[:, sl], cosf, sins) * AB_HEAD_DIM ** -0.5
        vh = _bf(v_ref[:, sl])
        sh = state[h]
        inner = _dot_nt(_bf(qh), _bf(kh)) * dmat_ref[h]
        o = _dot(_bf(inner), vh) + _dot(_bf(qh * cross_ref[:, sl]), _bf(sh))
        kd = kh * kdec_ref[:, sl]
        state[h] = sdec_ref[h] * sh + _dot(_bf(kd.T), vh)
        o_ref[:, sl] = _head_norm_gate(o, g_ref[:, sl])
    s_ref[...] = state[...]


def _retention_prompt(h, t):
    lc = RET_CHUNK
    lg = _ret_log_decay()
    i = jnp.arange(lc, dtype=F32)
    diff = i[:, None] - i[None, :]
    dmat = jnp.where(diff >= 0, jnp.exp(jnp.maximum(diff, 0.0)[None] * lg[:, None, None]), 0.0)
    cross = jnp.repeat(jnp.exp((i + 1.0)[:, None] * lg[None, :]), AB_HEAD_DIM, axis=1)
    kdec = jnp.repeat(jnp.exp((lc - 1.0 - i)[:, None] * lg[None, :]), AB_HEAD_DIM, axis=1)
    sdec = jnp.broadcast_to(jnp.exp(lc * lg)[:, None, None], (RET_HEADS, AB_HEAD_DIM, AB_HEAD_DIM))
    col = lambda c: pl.BlockSpec((lc, AB_WIDTH), lambda n, c=c: (n, c))
    const2 = pl.BlockSpec((lc, AB_WIDTH), lambda n: (0, 0))
    const3 = pl.BlockSpec((RET_HEADS, AB_HEAD_DIM, AB_HEAD_DIM), lambda n: (0, 0, 0))
    return pl.pallas_call(
        _ret_kernel, grid=(t // lc,),
        in_specs=[col(0), col(1), col(2), col(3), const3, const2, const2, const3],
        out_specs=[pl.BlockSpec((lc, AB_WIDTH), lambda n: (n, 0)), const3],
        out_shape=[jax.ShapeDtypeStruct((t, AB_WIDTH), F32),
                   jax.ShapeDtypeStruct((RET_HEADS, AB_HEAD_DIM, AB_HEAD_DIM), F32)],
        scratch_shapes=[pltpu.VMEM((RET_HEADS, AB_HEAD_DIM, AB_HEAD_DIM), F32)],
        compiler_params=_cparams("arbitrary"), name="retention_prompt")(
            h, h, h, h, dmat, cross, kdec, sdec)


def _ret_step_kernel(q_ref, k_ref, v_ref, g_ref, gam_ref, s_ref, o_ref, so_ref, *, pos0):
    d = AB_HEAD_DIM
    cosf, sins = _rope_tables(jnp.full((1, d), pos0, jnp.int32))
    eye = _iota((d, d), 0) == _iota((d, d), 1)
    for h in range(RET_HEADS):
        sl = slice(h * d, (h + 1) * d)
        qh = _rot(q_ref[0][:, sl], cosf, sins)
        kh = _rot(k_ref[0][:, sl], cosf, sins) * d ** -0.5
        vh = v_ref[0][:, sl]
        gam = gam_ref[h:h + 1, :]
        qcol = jnp.sum(jnp.where(eye, jnp.broadcast_to(qh, (d, d)), 0.0), axis=1, keepdims=True)
        kcol = jnp.sum(jnp.where(eye, jnp.broadcast_to(kh, (d, d)), 0.0), axis=1, keepdims=True)
        sh = s_ref[0, h]
        qk = jnp.sum(qh * kh, axis=1, keepdims=True)
        o = qk * vh + jnp.sum((qcol * gam) * sh, axis=0, keepdims=True)
        so_ref[0, h] = gam * sh + kcol * vh
        o_ref[0, :, sl] = _head_norm_gate(o, g_ref[0][:, sl])


def _retention_step(h_s, state, pos0):
    b = h_s.shape[0]
    gam = jnp.broadcast_to(jnp.exp(_ret_log_decay())[:, None], (RET_HEADS, AB_HEAD_DIM))
    h3 = h_s.reshape(b, 1, h_s.shape[1])
    col = lambda c: pl.BlockSpec((1, 1, AB_WIDTH), lambda n, c=c: (n, 0, c))
    sspec = pl.BlockSpec((1, RET_HEADS, AB_HEAD_DIM, AB_HEAD_DIM), lambda n: (n, 0, 0, 0))
    o, s_new = pl.pallas_call(
        functools.partial(_ret_step_kernel, pos0=pos0), grid=(b,),
        in_specs=[col(0), col(1), col(2), col(3),
                  pl.BlockSpec((RET_HEADS, AB_HEAD_DIM), lambda n: (0, 0)), sspec],
        out_specs=[pl.BlockSpec((1, 1, AB_WIDTH), lambda n: (n, 0, 0)), sspec],
        out_shape=[jax.ShapeDtypeStruct((b, 1, AB_WIDTH), F32),
                   jax.ShapeDtypeStruct(state.shape, F32)],
        compiler_params=_cparams("parallel"), name="retention_step")(
            h3, h3, h3, h3, gam, state)
    return o.reshape(b, AB_WIDTH), s_new


def _causal_pairs(nq, tq, tk):
    qi, kj = [], []
    for i in range(nq):
        for j in range((i * tq + tq - 1) // tk + 1):
            qi.append(i)
            kj.append(j)
    return jnp.asarray(np.array(qi, np.int32)), jnp.asarray(np.array(kj, np.int32))


def _online_step(u, v_bf, m_old, l_old, acc_old):
    d = acc_old.shape[1]
    m_new = jnp.maximum(m_old, jnp.max(u, -1, keepdims=True))
    p = jnp.exp2(u - jnp.concatenate([m_new] * (u.shape[1] // LANE), 1))
    a = jnp.exp2(m_old - m_new)
    l_new = a * l_old + jnp.sum(p, -1, keepdims=True)
    acc_new = a[:, :d] * acc_old + _dot(_bf(p), v_bf)
    return m_new, l_new, acc_new


def _fox_kernel(qi_ref, kj_ref, q_ref, k_ref, v_ref, ck_ref, o_ref, m_sc, l_sc, acc_sc, *, tq, tk, rc):
    step = pl.program_id(0)
    i, j = qi_ref[step], kj_ref[step]
    d = AB_HEAD_DIM
    c = d ** -0.5 * LOG2E

    @pl.when(j == 0)
    def _():
        m_sc[...] = jnp.full_like(m_sc, NEG_INF)
        l_sc[...] = jnp.zeros_like(l_sc)
        acc_sc[...] = jnp.zeros_like(acc_sc)

    def attend(diagonal):
        for h in range(FOX_HEADS):
            sl = slice(h * d, (h + 1) * d)
            kh, vh = _bf(k_ref[:, sl]), _bf(v_ref[:, sl])
            ck = ck_ref[h:h + 1, :] * LOG2E
            m_all, l_all, acc_all = m_sc[h], l_sc[h], acc_sc[:, sl]
            new = []
            for r0 in range(0, tq, rc):
                rows = slice(r0, r0 + rc)
                u = _dot_nt(_bf(q_ref[rows, sl]), kh) * c - ck
                if diagonal:
                    u = jnp.where(_iota((rc, tk), 1) <= r0 + _iota((rc, tk), 0), u, NEG_INF)
                new.append(_online_step(u, vh, m_all[rows], l_all[rows], acc_all[rows]))
            m_sc[h] = jnp.concatenate([n[0] for n in new], 0)
            l_sc[h] = jnp.concatenate([n[1] for n in new], 0)
            acc_sc[:, sl] = jnp.concatenate([n[2] for n in new], 0)

    @pl.when(j < i)
    def _():
        attend(False)

    @pl.when(j == i)
    def _():
        attend(True)
        for h in range(FOX_HEADS):
            sl = slice(h * d, (h + 1) * d)
            o_ref[:, sl] = acc_sc[:, sl] / jnp.maximum(l_sc[h], 1e-30)


def _fox_prompt(h, c_t, t):
    tq = tk = min(512, t)
    qi, kj = _causal_pairs(t // tq, tq, tk)
    return pl.pallas_call(
        functools.partial(_fox_kernel, tq=tq, tk=tk, rc=tq),
        grid_spec=pltpu.PrefetchScalarGridSpec(
            num_scalar_prefetch=2, grid=(qi.shape[0],),
            in_specs=[pl.BlockSpec((tq, AB_WIDTH), lambda s, qi, kj: (qi[s], 4)),
                      pl.BlockSpec((tk, AB_WIDTH), lambda s, qi, kj: (kj[s], 5)),
                      pl.BlockSpec((tk, AB_WIDTH), lambda s, qi, kj: (kj[s], 6)),
                      pl.BlockSpec((8, tk), lambda s, qi, kj: (0, kj[s]))],
            out_specs=pl.BlockSpec((tq, AB_WIDTH), lambda s, qi, kj: (qi[s], 0)),
            scratch_shapes=[pltpu.VMEM((FOX_HEADS, tq, LANE), F32), pltpu.VMEM((FOX_HEADS, tq, LANE), F32),
                            pltpu.VMEM((tq, AB_WIDTH), F32)]),
        out_shape=jax.ShapeDtypeStruct((t, AB_WIDTH), F32),
        compiler_params=_cparams("arbitrary"), name="fox_flash")(qi, kj, h, h, h, c_t)


def _proj_ln_kernel(*refs, sizes):
    pos, acc = 0, None
    for n in sizes:
        a = refs[pos][...]
        for t in range(1, n):
            a = a + refs[pos + t][...]
        d = _dot(_bf(a), refs[pos + n][...])
        acc = d if acc is None else acc + d
        pos += n + 1
    x_ref, g_ref, b_ref, o_ref = refs[pos:pos + 4]
    o_ref[...] = _ln(ALPHA * x_ref[...] + acc, g_ref[...], b_ref[...])


def _proj_ln(groups, x, g, b, name):
    m, dm = x.shape
    tm = min(512, m)
    args, specs, sizes = [], [], []
    for arrs, w in groups:
        sizes.append(len(arrs))
        for a in arrs:
            args.append(a)
            specs.append(pl.BlockSpec((tm, a.shape[1]), lambda i: (i, 0)))
        args.append(w)
        specs.append(pl.BlockSpec(w.shape, lambda i: (0, 0)))
    vec = pl.BlockSpec((1, dm), lambda i: (0, 0))
    row = pl.BlockSpec((tm, dm), lambda i: (i, 0))
    return pl.pallas_call(
        functools.partial(_proj_ln_kernel, sizes=tuple(sizes)), grid=(m // tm,),
        in_specs=specs + [row, vec, vec], out_specs=row,
        out_shape=jax.ShapeDtypeStruct((m, dm), F32),
        compiler_params=_cparams("parallel"), name=name)(*args, x, g.reshape(1, dm), b.reshape(1, dm))


def _mem_kernel(x_ref, wq_ref, mk_ref, mv_ref, wo_ref, g_ref, b_ref, o_ref):
    x = x_ref[...]
    q = _dot(_bf(x), wq_ref[...])
    dh = q.shape[1] // MEM_HEADS
    outs = []
    for h in range(MEM_HEADS):
        sl = slice(h * dh, (h + 1) * dh)
        s = _dot_nt(_bf(q[:, sl]), mk_ref[:, sl]) * dh ** -0.5
        e = jnp.exp(s - jnp.max(s, -1, keepdims=True))
        p = e / jnp.sum(e, -1, keepdims=True)
        outs.append(_dot(_bf(p), mv_ref[:, sl]))
    y = _dot(_bf(jnp.concatenate(outs, -1)), wo_ref[...])
    o_ref[...] = _ln(ALPHA * x + y, g_ref[...], b_ref[...])


def _mem_prompt(x, wq, mk_bf, mv_bf, wo, g, b):
    m, dm = x.shape
    tm = min(256, m)
    full = lambda a: pl.BlockSpec(a.shape, lambda i: (0, 0))
    vec = pl.BlockSpec((1, dm), lambda i: (0, 0))
    row = pl.BlockSpec((tm, dm), lambda i: (i, 0))
    return pl.pallas_call(
        _mem_kernel, grid=(m // tm,),
        in_specs=[row, full(wq), full(mk_bf), full(mv_bf), full(wo), vec, vec], out_specs=row,
        out_shape=jax.ShapeDtypeStruct((m, dm), F32),
        compiler_params=_cparams("parallel"), name="mem_attn_prompt")(
            x, wq, mk_bf, mv_bf, wo, g.reshape(1, dm), b.reshape(1, dm))


def _router_kernel(x_ref, w_ref, b_ref, pk_ref, cnt_ref, carry):
    @pl.when(pl.program_id(0) == 0)
    def _():
        carry[...] = jnp.zeros_like(carry)

    z = _dot_hi(x_ref[...], w_ref[...]) + b_ref[...]
    tm = z.shape[0]
    lane = _iota((tm, LANE), 1)
    lanef = lane.astype(F32)
    big = float(LANE)

    def first_max(v):
        mx = jnp.max(v, -1, keepdims=True)
        return mx, jnp.min(jnp.where(v == mx, lanef, big), -1, keepdims=True)

    gmask = lane < N_GROUPS
    gmax, gsel = first_max(jnp.where(gmask, z, -jnp.inf))
    g_w = 1.0 / jnp.sum(jnp.where(gmask, jnp.exp(z - gmax), 0.0), -1, keepdims=True)
    lo = N_GROUPS + EXP_PER_GROUP * gsel
    ze = jnp.where((lanef >= lo) & (lanef < lo + EXP_PER_GROUP), z, -jnp.inf)
    v1, i1 = first_max(ze)
    v2, i2 = first_max(jnp.where(lanef == i1, -jnp.inf, ze))
    e2 = jnp.exp(v2 - v1)
    w1 = (1.0 / (1.0 + e2)) * g_w
    w2 = (e2 / (1.0 + e2)) * g_w
    id1, id2 = i1 - N_GROUPS, i2 - N_GROUPS
    oh1 = (lanef == id1).astype(F32)
    oh2 = (lanef == id2).astype(F32)
    both = oh1 + oh2
    tri = (_iota((tm, tm), 1) < _iota((tm, tm), 0)).astype(BF16)
    cnt = _dot(tri, _bf(both)) + carry[...]
    r1 = jnp.sum(cnt * oh1, -1, keepdims=True)
    r2 = jnp.sum(cnt * oh2, -1, keepdims=True) + jnp.sum(oh1 * oh2, -1, keepdims=True)
    carry[...] = carry[...] + jnp.sum(both, axis=0, keepdims=True)
    cnt_ref[...] = jnp.broadcast_to(carry[...], cnt_ref.shape)
    pk = jnp.zeros((tm, LANE), F32)
    for k, v in enumerate((id1, id2, w1, w2, r1, r2)):
        pk = jnp.where(lane == k, v, pk)
    pk_ref[...] = pk


def _router(x, w_r, b_r):
    m, dm = x.shape
    tm = min(256, m)
    return pl.pallas_call(
        _router_kernel, grid=(m // tm,),
        in_specs=[pl.BlockSpec((tm, dm), lambda i: (i, 0)),
                  pl.BlockSpec((dm, LANE), lambda i: (0, 0)),
                  pl.BlockSpec((1, LANE), lambda i: (0, 0))],
        out_specs=[pl.BlockSpec((tm, LANE), lambda i: (i, 0)),
                   pl.BlockSpec((8, LANE), lambda i: (0, 0))],
        out_shape=[jax.ShapeDtypeStruct((m, LANE), F32), jax.ShapeDtypeStruct((8, LANE), F32)],
        scratch_shapes=[pltpu.VMEM((1, LANE), F32)],
        compiler_params=_cparams("arbitrary"), name="moe_router")(x, w_r, b_r)


def _row_copy(src, s, dst, d, sem):
    return pltpu.make_async_copy(src.at[pl.ds(s, 1), :], dst.at[pl.ds(d, 1), :], sem)


def _dispatch_kernel(dest_ref, x_ref, zin_ref, xin_ref, sem, *, tm):
    del zin_ref
    base = pl.program_id(0) * tm * 2

    def issue(r, c):
        _row_copy(x_ref, r, xin_ref, dest_ref[base + 2 * r], sem).start()
        _row_copy(x_ref, r, xin_ref, dest_ref[base + 2 * r + 1], sem).start()
        return c

    def drain(r, c):
        _row_copy(x_ref, 0, xin_ref, 0, sem).wait()
        _row_copy(x_ref, 0, xin_ref, 0, sem).wait()
        return c

    lax.fori_loop(0, tm, issue, 0)
    lax.fori_loop(0, tm, drain, 0)


def _dispatch(x, dest_flat, n_rows):
    m, dm = x.shape
    tm = min(256, m)
    return pl.pallas_call(
        functools.partial(_dispatch_kernel, tm=tm),
        grid_spec=pltpu.PrefetchScalarGridSpec(
            num_scalar_prefetch=1, grid=(m // tm,),
            in_specs=[pl.BlockSpec((tm, dm), lambda i, d: (i, 0)),
                      pl.BlockSpec(memory_space=pl.ANY)],
            out_specs=pl.BlockSpec(memory_space=pl.ANY),
            scratch_shapes=[pltpu.SemaphoreType.DMA(())]),
        out_shape=jax.ShapeDtypeStruct((n_rows, dm), F32),
        input_output_aliases={2: 0},
        compiler_params=_cparams("arbitrary"), name="moe_dispatch")(
            dest_flat, x, jnp.zeros((n_rows, dm), F32))


def _expert_kernel(be_ref, nu_ref, x_ref, w1_ref, w3_ref, w2_ref, o_ref, w1b, w3b, w2b):
    i = pl.program_id(0)
    changed = (i == 0) | (be_ref[i] != be_ref[jnp.maximum(i - 1, 0)])

    @pl.when(changed)
    def _():
        w1b[...] = _bf(w1_ref[0, 0])
        w3b[...] = _bf(w3_ref[0, 0])
        w2b[...] = _bf(w2_ref[0, 0])

    @pl.when(i < nu_ref[0])
    def _():
        x = _bf(x_ref[...])
        a = _dot(x, w1b[...])
        hid = (a * (1.0 / (1.0 + jnp.exp(-a)))) * _dot(x, w3b[...])
        o_ref[...] = _dot(_bf(hid), w2b[...])

    @pl.when(i >= nu_ref[0])
    def _():
        o_ref[...] = jnp.zeros_like(o_ref)


def _experts(xin, blk_e, n_used, w1, w3, w2, layer, blk):
    n_rows, dm = xin.shape
    de = w1.shape[-1]
    wspec = lambda shp: pl.BlockSpec((1, 1) + shp, lambda i, be, nu: (layer, be[i], 0, 0))
    return pl.pallas_call(
        _expert_kernel,
        grid_spec=pltpu.PrefetchScalarGridSpec(
            num_scalar_prefetch=2, grid=(n_rows // blk,),
            in_specs=[pl.BlockSpec((blk, dm), lambda i, be, nu: (i, 0)),
                      wspec((dm, de)), wspec((dm, de)), wspec((de, dm))],
            out_specs=pl.BlockSpec((blk, dm), lambda i, be, nu: (i, 0)),
            scratch_shapes=[pltpu.VMEM((dm, de), BF16), pltpu.VMEM((dm, de), BF16),
                            pltpu.VMEM((de, dm), BF16)]),
        out_shape=jax.ShapeDtypeStruct((n_rows, dm), F32),
        compiler_params=_cparams("arbitrary"), name="moe_experts")(blk_e, n_used, xin, w1, w3, w2)


def _combine_kernel(dest_ref, yb_ref, pk_ref, x_ref, g_ref, b_ref, o_ref, buf0, buf1, sem, *, tm):
    base = pl.program_id(0) * tm * 2

    def issue(r, c):
        _row_copy(yb_ref, dest_ref[base + 2 * r], buf0, r, sem).start()
        _row_copy(yb_ref, dest_ref[base + 2 * r + 1], buf1, r, sem).start()
        return c

    def drain(r, c):
        _row_copy(yb_ref, 0, buf0, 0, sem).wait()
        _row_copy(yb_ref, 0, buf1, 0, sem).wait()
        return c

    lax.fori_loop(0, tm, issue, 0)
    lax.fori_loop(0, tm, drain, 0)
    pk = pk_ref[...]
    y = buf0[...] * pk[:, 2:3] + buf1[...] * pk[:, 3:4]
    o_ref[...] = _ln(ALPHA * x_ref[...] + y, g_ref[...], b_ref[...])


def _combine(yb, dest_flat, pk, x, g, b):
    m, dm = x.shape
    tm = min(128, m)
    vec = pl.BlockSpec((1, dm), lambda i, d: (0, 0))
    row = pl.BlockSpec((tm, dm), lambda i, d: (i, 0))
    return pl.pallas_call(
        functools.partial(_combine_kernel, tm=tm),
        grid_spec=pltpu.PrefetchScalarGridSpec(
            num_scalar_prefetch=1, grid=(m // tm,),
            in_specs=[pl.BlockSpec(memory_space=pl.ANY),
                      pl.BlockSpec((tm, LANE), lambda i, d: (i, 0)), row, vec, vec],
            out_specs=row,
            scratch_shapes=[pltpu.VMEM((tm, dm), F32), pltpu.VMEM((tm, dm), F32),
                            pltpu.SemaphoreType.DMA(())]),
        out_shape=jax.ShapeDtypeStruct((m, dm), F32),
        compiler_params=_cparams("arbitrary"), name="moe_combine")(
            dest_flat, yb, pk, x, g.reshape(1, dm), b.reshape(1, dm))


def _hier_moe_ln(x, layer, blk, w_rg, b_rg, w_re, b_re, w_e1, w_e3, w_e2, g, b):
    m, dm = x.shape
    w_r = _pad_cols(jnp.concatenate([w_rg[layer], w_re[layer]], 1))
    b_r = _pad_cols(jnp.concatenate([b_rg[layer], b_re[layer]])[None, :])
    pk, cnt = _router(x, w_r, b_r)
    e_id = pk[:, 0:2].astype(jnp.int32)
    rank = pk[:, 4:6].astype(jnp.int32)
    counts = cnt[0, :N_EXPERTS].astype(jnp.int32)
    pcounts = (counts + blk - 1) // blk * blk
    pends = jnp.cumsum(pcounts)
    pstarts = pends - pcounts
    dest = (pstarts[e_id] + rank).reshape(-1)
    n_rows = (-(-(2 * m) // blk) + N_EXPERTS) * blk
    n_blk = n_rows // blk
    blk_e = jnp.minimum(jnp.sum(pends[None, :] <= (jnp.arange(n_blk) * blk)[:, None], axis=1),
                        N_EXPERTS - 1).astype(jnp.int32)
    n_used = (pends[-1:] // blk).astype(jnp.int32)
    xin = _dispatch(x, dest, n_rows)
    yb = _experts(xin, blk_e, n_used, w_e1, w_e3, w_e2, layer, blk)
    return _combine(yb, dest, pk, x, g, b)


def _compress_kernel(*refs, n_pages, per_page, n_prefetch=0):
    refs = refs[n_prefetch:]
    ak, av, wk, wv, ok, ov = refs[2 * n_pages:]
    for pages, a_ref, w_ref, o_ref in ((refs[:n_pages], ak, wk, ok), (refs[n_pages:2 * n_pages], av, wv, ov)):
        pooled = []
        for p_ref in pages:
            x = p_ref[...].reshape(per_page, CMP_BLOCK, LANE) * a_ref[...][None]
            pooled.append(jnp.sum(x, axis=1))
        pooled = pooled[0] if n_pages == 1 else jnp.concatenate(pooled, 0)
        o_ref[...] = _dot(_bf(pooled), w_ref[...]).reshape(o_ref.shape)


def _compress_consts(a_ck, a_cv, w_ck, w_cv):
    eye = jnp.eye(NSA_KV_HEADS, dtype=F32)
    return (jnp.tile(a_ck, (1, NSA_KV_HEADS)), jnp.tile(a_cv, (1, NSA_KV_HEADS)),
            _bf(jnp.kron(eye, w_ck)), _bf(jnp.kron(eye, w_cv)))


def _compress_prompt(h1, t, consts):
    rows = min(2048, t)
    per = rows // CMP_BLOCK
    c2 = lambda a: pl.BlockSpec(a.shape, lambda i: (0, 0))
    kcol = NSA_WIDTH // LANE
    out = pl.BlockSpec((per, LANE), lambda i: (i, 0))
    return pl.pallas_call(
        functools.partial(_compress_kernel, n_pages=1, per_page=per), grid=(t // rows,),
        in_specs=[pl.BlockSpec((rows, LANE), lambda i: (i, kcol)),
                  pl.BlockSpec((rows, LANE), lambda i: (i, kcol + 1))] + [c2(a) for a in consts],
        out_specs=[out, out],
        out_shape=[jax.ShapeDtypeStruct((t // CMP_BLOCK, LANE), F32)] * 2,
        compiler_params=_cparams("parallel"), name="nsa_compress_prompt")(h1, h1, *consts)


def _compress_paged(pool_k, pool_v, table, consts):
    b, n_pages = table.shape
    pp = min(16, n_pages)
    per = PAGE_SIZE // CMP_BLOCK
    page = lambda p: pl.BlockSpec((None, PAGE_SIZE, LANE), lambda n, j, t, p=p: (t[n, j * pp + p], 0, 0))
    c2 = lambda a: pl.BlockSpec(a.shape, lambda n, j, t: (0, 0))
    out = pl.BlockSpec((1, pp * per, LANE), lambda n, j, t: (n, j, 0))
    nblk = n_pages * per
    return pl.pallas_call(
        functools.partial(_compress_kernel, n_pages=pp, per_page=per, n_prefetch=1),
        grid_spec=pltpu.PrefetchScalarGridSpec(
            num_scalar_prefetch=1, grid=(b, n_pages // pp),
            in_specs=[page(p) for p in range(pp)] * 2 + [c2(a) for a in consts],
            out_specs=[out, out]),
        out_shape=[jax.ShapeDtypeStruct((b, nblk, LANE), F32)] * 2,
        compiler_params=_cparams("parallel", "arbitrary"), name="nsa_compress_paged")(
            table, *([pool_k] * pp), *([pool_v] * pp), *consts)


def _select_mask(score, n_pick):
    n = score.shape[1]
    lanef = _iota(score.shape, 1).astype(F32)

    def body(_, carry):
        sel, sc = carry
        mx = jnp.max(sc, -1, keepdims=True)
        idx = jnp.min(jnp.where(sc == mx, lanef, float(n)), -1, keepdims=True)
        hit = lanef == idx
        return jnp.where(hit, 1.0, sel), jnp.where(hit, -2.0, sc)

    sel, _ = lax.fori_loop(0, n_pick, body, (jnp.zeros_like(score), score))
    return sel


def _stack_heads(q_ref, g):
    d = NSA_HEAD_DIM
    return jnp.concatenate(
        [q_ref[:, (g * NSA_GROUP + r) * d:(g * NSA_GROUP + r + 1) * d] for r in range(NSA_GROUP)], 0)


def _cmp_select_kernel(q_ref, kc_ref, vc_ref, gate_ref, o_ref, sel_ref, *, tq, nbp):
    i = pl.program_id(0)
    d = NSA_HEAD_DIM
    rows = NSA_GROUP * tq
    t_rows = i * tq + (_iota((rows, nbp), 0) & (tq - 1))
    blk_rows = _iota((rows, nbp), 1)
    cmask = (blk_rows + 1) * CMP_BLOCK - 1 <= t_rows
    t = i * tq + _iota((tq, nbp), 0)
    blk = _iota((tq, nbp), 1)
    cur = t >> 6
    forced = (blk == 0) | (blk == cur) | (blk == cur - 1)
    future = blk * CMP_BLOCK > t
    for g in range(NSA_KV_HEADS):
        sl = slice(g * d, (g + 1) * d)
        qs = _bf(_stack_heads(q_ref, g))
        logits = jnp.where(cmask, _dot_nt(qs, _bf(kc_ref[:, sl])) * d ** -0.5, NEG_INF)
        p = jnp.where(cmask, jnp.exp(logits - jnp.max(logits, -1, keepdims=True)), 0.0)
        p = p / jnp.maximum(jnp.sum(p, -1, keepdims=True), 1e-30)
        o = _dot(_bf(p), _bf(vc_ref[:, sl]))
        imp = p[0:tq]
        for r in range(1, NSA_GROUP):
            imp = imp + p[r * tq:(r + 1) * tq]
        score = jnp.where(forced, SELECT_FORCE, jnp.where(future, -1.0, imp))
        sel_ref[:, g * nbp:(g + 1) * nbp] = _bf((_select_mask(score, N_SELECT) - 1.0) * -NEG_INF)
        for r in range(NSA_GROUP):
            hh = g * NSA_GROUP + r
            o_ref[:, hh * d:(hh + 1) * d] = o[r * tq:(r + 1) * tq] * gate_ref[:, hh * 3:hh * 3 + 1]


def _cmp_select_prompt(h1, kc_c, vc_c, gates, t, nbp):
    tq = min(128, t)
    c2 = lambda a: pl.BlockSpec(a.shape, lambda i: (0, 0))
    return pl.pallas_call(
        functools.partial(_cmp_select_kernel, tq=tq, nbp=nbp), grid=(t // tq,),
        in_specs=[pl.BlockSpec((tq, NSA_WIDTH), lambda i: (i, 0)), c2(kc_c), c2(vc_c),
                  pl.BlockSpec((tq, LANE), lambda i: (i, 0))],
        out_specs=[pl.BlockSpec((tq, NSA_WIDTH), lambda i: (i, 0)),
                   pl.BlockSpec((tq, NSA_KV_HEADS * nbp), lambda i: (i, 0))],
        out_shape=[jax.ShapeDtypeStruct((t, NSA_WIDTH), F32),
                   jax.ShapeDtypeStruct((t, NSA_KV_HEADS * nbp), BF16)],
        compiler_params=_cparams("parallel"), name="nsa_cmp_select_prompt")(h1, kc_c, vc_c, gates)


def _write_gated_heads(o_ref, gate_ref, g, o, tq, gate_j):
    d = NSA_HEAD_DIM
    for r in range(NSA_GROUP):
        hh = g * NSA_GROUP + r
        o_ref[:, hh * d:(hh + 1) * d] = (
            o[r * tq:(r + 1) * tq] * gate_ref[:, hh * 3 + gate_j:hh * 3 + gate_j + 1])


def _sel_flash_kernel(qi_ref, kj_ref, q_ref, k_ref, v_ref, selb_ref, gate_ref, o_ref,
                      qs, m_sc, l_sc, acc_sc, *, tq, tk, nbp):
    step = pl.program_id(0)
    i, j = qi_ref[step], kj_ref[step]
    d = NSA_HEAD_DIM
    c = d ** -0.5 * LOG2E

    @pl.when(j == 0)
    def _():
        m_sc[...] = jnp.full_like(m_sc, NEG_INF)
        l_sc[...] = jnp.zeros_like(l_sc)
        acc_sc[...] = jnp.zeros_like(acc_sc)
        for g in range(NSA_KV_HEADS):
            qs[g] = _bf(_stack_heads(q_ref, g))

    per_tile = tk // CMP_BLOCK
    expand = (_iota((nbp, tk), 0) == j * per_tile + (_iota((nbp, tk), 1) >> 6)).astype(BF16)
    causal = jnp.where(j * tk + _iota((tq, tk), 1) <= i * tq + _iota((tq, tk), 0), 0.0, NEG_INF)
    for g in range(NSA_KV_HEADS):
        sl = slice(g * d, (g + 1) * d)
        bias = _dot(selb_ref[:, g * nbp:(g + 1) * nbp], expand) + causal
        kg, vg = _bf(k_ref[:, sl]), _bf(v_ref[:, sl])
        u = _dot_nt(qs[g], kg) * c + jnp.concatenate([bias] * NSA_GROUP, 0)
        m_sc[g], l_sc[g], acc_sc[g] = _online_step(u, vg, m_sc[g], l_sc[g], acc_sc[g])

    @pl.when(j == (i * tq + tq - 1) // tk)
    def _():
        for g in range(NSA_KV_HEADS):
            l_fin = jnp.maximum(l_sc[g][:, :d], 1e-30)
            _write_gated_heads(o_ref, gate_ref, g, acc_sc[g] / l_fin, tq, 1)


def _sel_flash_prompt(h1, selb, gates, t, nbp):
    tq = min(128, t)
    tk = min(512, t)
    kcol = NSA_WIDTH // LANE
    qi, kj = _causal_pairs(t // tq, tq, tk)
    rows = NSA_GROUP * tq
    qrow = lambda w: pl.BlockSpec((tq, w), lambda s, qi, kj: (qi[s], 0))
    kv = lambda c: pl.BlockSpec((tk, LANE), lambda s, qi, kj, c=c: (kj[s], c))
    return pl.pallas_call(
        functools.partial(_sel_flash_kernel, tq=tq, tk=tk, nbp=nbp),
        grid_spec=pltpu.PrefetchScalarGridSpec(
            num_scalar_prefetch=2, grid=(qi.shape[0],),
            in_specs=[qrow(NSA_WIDTH), kv(kcol + 2), kv(kcol + 3), qrow(NSA_KV_HEADS * nbp), qrow(LANE)],
            out_specs=qrow(NSA_WIDTH),
            scratch_shapes=[pltpu.VMEM((NSA_KV_HEADS, rows, NSA_HEAD_DIM), BF16),
                            pltpu.VMEM((NSA_KV_HEADS, rows, LANE), F32),
                            pltpu.VMEM((NSA_KV_HEADS, rows, LANE), F32),
                            pltpu.VMEM((NSA_KV_HEADS, rows, NSA_HEAD_DIM), F32)]),
        out_shape=jax.ShapeDtypeStruct((t, NSA_WIDTH), F32),
        compiler_params=_cparams("arbitrary"), name="nsa_flash_sel")(qi, kj, h1, h1, h1, selb, gates)


def _win_kernel(*refs, tq, nw):
    q_ref = refs[0]
    k_refs, v_refs = refs[1:1 + nw], refs[1 + nw:1 + 2 * nw]
    gate_ref, o_ref = refs[1 + 2 * nw:]
    i = pl.program_id(0)
    d = NSA_HEAD_DIM
    c = d ** -0.5 * LOG2E
    w = nw * tq
    rel = _iota((tq, w), 0) + (nw - 1) * tq - _iota((tq, w), 1)
    kpos = (i - nw + 1) * tq + _iota((tq, w), 1)
    bias = jnp.where((rel >= 0) & (rel <= WINDOW) & (kpos >= 0), 0.0, NEG_INF)
    for g in range(NSA_KV_HEADS):
        sl = slice(g * d, (g + 1) * d)
        kg = _bf(jnp.concatenate([k[:, sl] for k in k_refs], 0))
        vg = _bf(jnp.concatenate([v[:, sl] for v in v_refs], 0))
        u = _dot_nt(_bf(_stack_heads(q_ref, g)), kg) * c + jnp.concatenate([bias] * NSA_GROUP, 0)
        p = jnp.exp2(u - jnp.max(u, -1, keepdims=True))
        o = _dot(_bf(p), vg) / jnp.sum(p, -1, keepdims=True)
        _write_gated_heads(o_ref, gate_ref, g, o, tq, 2)


def _win_prompt(h1, gates, t):
    tq = min(128, t)
    nw = WINDOW // tq + 1
    kcol = NSA_WIDTH // LANE
    kv = lambda c, p: pl.BlockSpec((tq, LANE), lambda i, c=c, p=p: (jnp.maximum(i - nw + 1 + p, 0), c))
    return pl.pallas_call(
        functools.partial(_win_kernel, tq=tq, nw=nw), grid=(t // tq,),
        in_specs=([pl.BlockSpec((tq, NSA_WIDTH), lambda i: (i, 0))]
                  + [kv(kcol + 4, p) for p in range(nw)] + [kv(kcol + 5, p) for p in range(nw)]
                  + [pl.BlockSpec((tq, LANE), lambda i: (i, 0))]),
        out_specs=pl.BlockSpec((tq, NSA_WIDTH), lambda i: (i, 0)),
        out_shape=jax.ShapeDtypeStruct((t, NSA_WIDTH), F32),
        compiler_params=_cparams("parallel"), name="nsa_window")(h1, *([h1] * (2 * nw)), gates)


def _page_cumsum_kernel(x_ref, o_ref):
    n = x_ref.shape[1]
    tri = (_iota((n, n), 0) <= _iota((n, n), 1)).astype(F32)
    o_ref[...] = _dot_hi(x_ref[...], tri)


def _page_cumsum(x):
    m, n = x.shape
    tm = 1024
    while m % tm:
        tm //= 2
    return pl.pallas_call(
        _page_cumsum_kernel, grid=(m // tm,),
        in_specs=[pl.BlockSpec((tm, n), lambda i: (i, 0))],
        out_specs=pl.BlockSpec((tm, n), lambda i: (i, 0)),
        out_shape=jax.ShapeDtypeStruct((m, n), F32),
        compiler_params=_cparams("parallel"), name="fox_page_cumsum")(x)


def _page_scores(q, k_ref, heads, transposed):
    if heads == 1:
        return _dot_nt(q, _bf(k_ref[...]))
    d = q.shape[1] // heads
    out = None
    for h in range(heads):
        qh = q[:, h * d:(h + 1) * d]
        part = _dot(qh, _bf(k_ref[h])) if transposed else _dot_nt(qh, _bf(k_ref[:, h, :]))
        out = part if out is None else out + part
    return out


def _page_values(p_bf, v_ref, heads, transposed):
    if heads == 1:
        return _dot(p_bf, _bf(v_ref[...]))
    if transposed:
        return jnp.concatenate([_dot_nt(p_bf, _bf(v_ref[h])) for h in range(heads)], 1)
    return jnp.concatenate([_dot(p_bf, _bf(v_ref[:, h, :])) for h in range(heads)], 1)


def _decode_kernel(*refs, pp, scale, bias, masked, new, scaled, nbp, heads, transposed):
    table_ref = refs[0]
    del table_ref
    pos = 1
    q_ref = refs[pos]; pos += 1
    k_refs = refs[pos:pos + pp]; pos += pp
    v_refs = refs[pos:pos + pp]; pos += pp
    if bias:
        c_refs = refs[pos:pos + pp]; pos += pp
    if masked:
        sel_ref = refs[pos]; pos += 1
    if new:
        kn_ref, vn_ref = refs[pos:pos + 2]; pos += 2
    if bias:
        bn_ref = refs[pos]; pos += 1
    if scaled:
        rs_ref = refs[pos]; pos += 1
    o_ref = refs[pos]; pos += 1
    m_sc, l_sc, acc_sc, carry = refs[pos:pos + 4]
    j = pl.program_id(1)
    nj = pl.num_programs(1)

    @pl.when(j == 0)
    def _():
        m_sc[...] = jnp.full_like(m_sc, NEG_INF)
        l_sc[...] = jnp.zeros_like(l_sc)
        acc_sc[...] = jnp.zeros_like(acc_sc)
        carry[...] = jnp.zeros_like(carry)

    q = _bf(q_ref[0])
    s = jnp.concatenate([_page_scores(q, k, heads, transposed) for k in k_refs], 1) * scale
    if bias:
        run = carry[...]
        cs = []
        for c in c_refs:
            cin = c[...]
            cs.append(run + cin)
            run = run + cin[:, PAGE_SIZE - 1:PAGE_SIZE]
        carry[...] = run
        s = s - jnp.concatenate(cs, 1)
    if masked:
        w = pp * PAGE_SIZE
        per = w // CMP_BLOCK
        expand = (_iota((nbp, w), 0) == j * per + (_iota((nbp, w), 1) >> 6)).astype(BF16)
        mask = _dot(_bf(sel_ref[0]), expand) > 0.5
        s = jnp.where(mask, s, NEG_INF)
    m_old = m_sc[...]
    m_new = jnp.maximum(m_old, jnp.max(s, -1, keepdims=True))
    p = jnp.exp(s - m_new)
    if masked:
        p = jnp.where(mask, p, 0.0)
    a = jnp.exp(m_old - m_new)
    l_sc[...] = a * l_sc[...] + jnp.sum(p, -1, keepdims=True)
    pv = None
    for n, v in enumerate(v_refs):
        d = _page_values(_bf(p[:, n * PAGE_SIZE:(n + 1) * PAGE_SIZE]), v, heads, transposed)
        pv = d if pv is None else pv + d
    acc_sc[...] = a * acc_sc[...] + pv
    m_sc[...] = m_new

    @pl.when(j == nj - 1)
    def _():
        m1, l1, acc = m_sc[...], l_sc[...], acc_sc[...]
        if new:
            qf = q.astype(F32)
            sn = jnp.sum(qf * _bf(kn_ref[0]).astype(F32), -1, keepdims=True) * scale
            if bias:
                sn = sn - (carry[...] + bn_ref[0][:, 0:1])
            m2 = jnp.maximum(m1, sn)
            a2 = jnp.exp(m1 - m2)
            pn = jnp.exp(sn - m2)
            l1 = a2 * l1 + pn
            acc = a2 * acc + pn * _bf(vn_ref[0]).astype(F32)
        out = acc / jnp.maximum(l1, 1e-30)
        if scaled:
            out = out * rs_ref[0][:, 0:1]
        o_ref[0] = out


def _decode(qrows, kpool, vpool, table, pp, scale, *, cpool=None, sel=None, knew=None, vnew=None,
            bnew=None, rowscale=None, tpages=0, name):
    b, r, dk = qrows.shape
    transposed = tpages > 0
    heads = (kpool.shape[1] if transposed else kpool.shape[2]) if kpool.ndim == 4 else 1
    dv = (vpool.shape[2] if transposed else vpool.shape[-1]) * heads
    n_pages = table.shape[1]
    pp = min(pp, n_pages)
    per_b = lambda shp: pl.BlockSpec((1,) + shp, lambda n, j, t: (n, 0, 0))
    page = lambda w, p, rws=PAGE_SIZE: pl.BlockSpec(
        (None, rws, w), lambda n, j, t, p=p: (t[n, j * pp + p], 0, 0))
    if transposed:
        kvpage = lambda w, p: pl.BlockSpec(
            (None, heads, w // heads, PAGE_SIZE),
            lambda n, j, t, p=p: (t[n, j * pp + p] // tpages, 0, 0, t[n, j * pp + p] % tpages))
    elif heads > 1:
        kvpage = lambda w, p: pl.BlockSpec(
            (None, PAGE_SIZE, heads, w // heads), lambda n, j, t, p=p: (t[n, j * pp + p], 0, 0, 0))
    else:
        kvpage = page
    args = [qrows] + [kpool] * pp + [vpool] * pp
    specs = [per_b((r, dk))] + [kvpage(dk, p) for p in range(pp)] + [kvpage(dv, p) for p in range(pp)]
    nbp = 0
    if cpool is not None:
        args += [cpool] * pp
        specs += [page(PAGE_SIZE, p, r) for p in range(pp)]
    if sel is not None:
        nbp = sel.shape[-1]
        args.append(sel)
        specs.append(per_b((r, nbp)))
    if knew is not None:
        args += [knew, vnew]
        specs += [per_b((1, dk)), per_b((1, dv))]
    if bnew is not None:
        args.append(bnew)
        specs.append(per_b((r, LANE)))
    if rowscale is not None:
        args.append(rowscale)
        specs.append(per_b((r, LANE)))
    kern = functools.partial(_decode_kernel, pp=pp, scale=scale, bias=cpool is not None,
                             masked=sel is not None, new=knew is not None,
                             scaled=rowscale is not None, nbp=nbp, heads=heads, transposed=transposed)
    return pl.pallas_call(
        kern,
        grid_spec=pltpu.PrefetchScalarGridSpec(
            num_scalar_prefetch=1, grid=(b, n_pages // pp), in_specs=specs,
            out_specs=per_b((r, dv)),
            scratch_shapes=[pltpu.VMEM((r, 1), F32), pltpu.VMEM((r, 1), F32),
                            pltpu.VMEM((r, dv), F32), pltpu.VMEM((r, 1), F32)]),
        out_shape=jax.ShapeDtypeStruct((b, r, dv), F32),
        compiler_params=_cparams("parallel", "arbitrary"), name=name)(table, *args)


def _contig_table(b, pages_per_seq):
    return (jnp.arange(b, dtype=jnp.int32)[:, None] * pages_per_seq
            + jnp.arange(pages_per_seq, dtype=jnp.int32)[None, :])


def _head_rows(x, n_heads, n_rows):
    b, w = x.shape
    d = w // n_heads
    hm = (jnp.arange(n_rows)[:, None] == (jnp.arange(w) // d)[None, :]).astype(x.dtype)
    return x[:, None, :] * hm[None]


def _head_diag(acc, n_heads):
    d = acc.shape[-1] // n_heads
    return jnp.concatenate([acc[:, h, h * d:(h + 1) * d] for h in range(n_heads)], -1)


def _nsa_rows(q):
    b = q.shape[0]
    qh = q.reshape(b, NSA_KV_HEADS, NSA_GROUP, 1, NSA_HEAD_DIM)
    eye = jnp.eye(NSA_KV_HEADS, dtype=q.dtype)[None, :, None, :, None]
    return (qh * eye).reshape(b, NSA_HEADS, NSA_KV_WIDTH)


def _nsa_diag(acc):
    b = acc.shape[0]
    a = acc.reshape(b, NSA_KV_HEADS, NSA_GROUP, NSA_KV_HEADS, NSA_HEAD_DIM)
    return jnp.concatenate([a[:, g, :, g, :] for g in range(NSA_KV_HEADS)], 1).reshape(b, NSA_WIDTH)


def _gate_rows(gates, j):
    b = gates.shape[0]
    g = gates[:, :NSA_HEADS * 3].reshape(b, NSA_HEADS, 3)[:, :, j]
    return jnp.broadcast_to(g[:, :, None], (b, NSA_HEADS, LANE))


def _cmp_select_step_kernel(q_ref, kc_ref, vc_ref, rs_ref, o_ref, sel_ref, *, nbp, n_vis):
    q = _bf(q_ref[0])
    blk = _iota((NSA_HEADS, nbp), 1)
    cmask = blk < n_vis
    logits = jnp.where(cmask, _dot_nt(q, _bf(kc_ref[0])) * NSA_HEAD_DIM ** -0.5, NEG_INF)
    p = jnp.where(cmask, jnp.exp(logits - jnp.max(logits, -1, keepdims=True)), 0.0)
    p = p / jnp.maximum(jnp.sum(p, -1, keepdims=True), 1e-30)
    o_ref[0] = _dot(_bf(p), _bf(vc_ref[0])) * rs_ref[0][:, 0:1]
    blk1 = _iota((1, nbp), 1)
    forced = (blk1 == 0) | (blk1 == n_vis - 1)
    for g in range(NSA_KV_HEADS):
        imp = jnp.sum(p[g * NSA_GROUP:(g + 1) * NSA_GROUP], axis=0, keepdims=True)
        score = jnp.where(forced, SELECT_FORCE, jnp.where(blk1 >= n_vis, -1.0, imp))
        sel = _select_mask(jnp.broadcast_to(score, (8, nbp)), N_SELECT - 1)[0:1]
        sel_ref[0, g * NSA_GROUP:(g + 1) * NSA_GROUP, :] = jnp.broadcast_to(sel, (NSA_GROUP, nbp))


def _cmp_select_step(qrows, kc_c, vc_c, rowscale, n_vis):
    b = qrows.shape[0]
    nbp = kc_c.shape[1]
    per_b = lambda shp: pl.BlockSpec((1,) + shp, lambda n: (n, 0, 0))
    return pl.pallas_call(
        functools.partial(_cmp_select_step_kernel, nbp=nbp, n_vis=n_vis), grid=(b,),
        in_specs=[per_b((NSA_HEADS, NSA_KV_WIDTH)), per_b((nbp, LANE)), per_b((nbp, LANE)),
                  per_b((NSA_HEADS, LANE))],
        out_specs=[per_b((NSA_HEADS, LANE)), per_b((NSA_HEADS, nbp))],
        out_shape=[jax.ShapeDtypeStruct((b, NSA_HEADS, LANE), F32),
                   jax.ShapeDtypeStruct((b, NSA_HEADS, nbp), F32)],
        compiler_params=_cparams("parallel"), name="nsa_cmp_select_step")(qrows, kc_c, vc_c, rowscale)


def kernel(x_prompt, x_sample, state_ret, cache_fox_k, cache_fox_v, cache_fox_logf, cache_cmp_k, cache_cmp_v, cache_slc_k, cache_slc_v, cache_win_k, cache_win_v, cache_mem_k, cache_mem_v, page_table, mem_prompt, w_in0, b_fox_f, w_out0, w_in1, a_ck, a_cv, w_ck, w_cv, w_out1, w_xq, w_xk, w_xv, w_xo, ln_m_g, ln_m_b, ln_x_g, ln_x_b, ln_f_g, ln_f_b, w_rg, b_rg, w_re, b_re, w_e1, w_e3, w_e2):
    bp, t, dm = x_prompt.shape
    bs, ls, _ = x_sample.shape
    assert bp == 1 and ls == 1
    n_phys = cache_fox_k.shape[0]
    n_pages = page_table.shape[1]
    past = n_pages * PAGE_SIZE
    win_buf = cache_win_k.shape[1]
    assert t % 512 == 0 and t // CMP_BLOCK >= N_SELECT and win_buf == WINDOW and past % CMP_BLOCK == 0
    page_table = page_table.astype(jnp.int32)
    y_p = x_prompt.reshape(t, dm)
    y_s = x_sample.reshape(bs, dm)

    n_mem = mem_prompt.shape[1]
    mem_k_list, mem_v_list = [], []

    def sublayers(y_p, y_s, mix_p_groups, mix_s_groups, layer):
        y_p = _proj_ln(mix_p_groups, y_p, ln_m_g[layer], ln_m_b[layer], "mix_out_ln_prompt")
        y_s = _proj_ln(mix_s_groups, y_s, ln_m_g[layer], ln_m_b[layer], "mix_out_ln_step")
        w_kv = _bf(jnp.concatenate([w_xk[layer], w_xv[layer]], 1))
        mkv = _matmul(mem_prompt.reshape(n_mem, dm), w_kv, n_mem, 512, "mem_kv")
        mk, mv = mkv[:, :dm], mkv[:, dm:]
        mem_k_list.append(mk.reshape(1, n_mem, MEM_HEADS, dm // MEM_HEADS))
        mem_v_list.append(mv.reshape(1, n_mem, MEM_HEADS, dm // MEM_HEADS))
        wq, wo = _bf(w_xq[layer]), _bf(w_xo[layer])
        y_p = _mem_prompt(y_p, wq, _bf(mk), _bf(mv), wo, ln_x_g[layer], ln_x_b[layer])
        q_s = _matmul(y_s, wq, bs, 512, "mem_q_step")
        mpages = n_mem // PAGE_SIZE
        acc = _decode(_head_rows(q_s, MEM_HEADS, DEC_ROWS),
                      cache_mem_k[layer].reshape(bs * mpages, PAGE_SIZE, dm),
                      cache_mem_v[layer].reshape(bs * mpages, PAGE_SIZE, dm),
                      _contig_table(bs, mpages), mpages, (dm // MEM_HEADS) ** -0.5, name="mem_attn_step")
        y_s = _proj_ln([([_head_diag(acc, MEM_HEADS)], wo)], y_s, ln_x_g[layer], ln_x_b[layer],
                       "mem_out_ln_step")
        moe = (w_rg, b_rg, w_re, b_re, w_e1, w_e3, w_e2, ln_f_g[layer], ln_f_b[layer])
        y_p = _hier_moe_ln(y_p, layer, 128, *moe)
        y_s = _hier_moe_ln(y_s, layer, 16, *moe)
        return y_p, y_s

    n_main = 7 * AB_WIDTH
    w_main = _bf(w_in0[:, :n_main])
    w_f = _bf(_pad_cols(w_in0[:, n_main:]))
    b_f = _pad_cols(b_fox_f[None, :])
    w0a, w0b = _bf(w_out0[:AB_WIDTH]), _bf(w_out0[AB_WIDTH:])
    h = _matmul(y_p, w_main, 1024, 512, "ab_proj_prompt")
    logf_p, c_p = _gate(y_p, w_f, b_f, "logsig_cum", "fox_gate_prompt")
    o_ret_p, ret_p = _retention_prompt(h, t)
    c_t = jnp.pad(c_p[:, :FOX_HEADS].T, ((0, 8 - FOX_HEADS), (0, 0)))
    o_fox_p = _fox_prompt(h, c_t, t)
    fk_p = h[:, 5 * AB_WIDTH:6 * AB_WIDTH].reshape(1, t, FOX_HEADS, AB_HEAD_DIM)
    fv_p = h[:, 6 * AB_WIDTH:7 * AB_WIDTH].reshape(1, t, FOX_HEADS, AB_HEAD_DIM)
    ff_p = logf_p[:, :FOX_HEADS].reshape(1, t, FOX_HEADS)
    h_s = _matmul(y_s, w_main, bs, 512, "ab_proj_step")
    logf_s = _gate(y_s, w_f, b_f, "logsig", "fox_gate_step")[0]
    o_ret_s, ret_s = _retention_step(h_s, state_ret, past)
    lf_t = jnp.pad(jnp.swapaxes(cache_fox_logf.astype(F32), 1, 2), ((0, 0), (0, DEC_ROWS - FOX_HEADS), (0, 0)))
    cpool = _page_cumsum(lf_t.reshape(n_phys * DEC_ROWS, PAGE_SIZE)).reshape(n_phys, DEC_ROWS, PAGE_SIZE)
    fq_s = h_s[:, 4 * AB_WIDTH:5 * AB_WIDTH]
    fk_s = h_s[:, 5 * AB_WIDTH:6 * AB_WIDTH]
    fv_s = h_s[:, 6 * AB_WIDTH:7 * AB_WIDTH]
    bnew = jnp.broadcast_to(
        jnp.pad(logf_s[:, :FOX_HEADS], ((0, 0), (0, DEC_ROWS - FOX_HEADS)))[:, :, None], (bs, DEC_ROWS, LANE))
    acc = _decode(_head_rows(fq_s, FOX_HEADS, DEC_ROWS),
                  cache_fox_k, cache_fox_v, page_table, 8, AB_HEAD_DIM ** -0.5, cpool=cpool,
                  knew=fk_s.reshape(bs, 1, AB_WIDTH), vnew=fv_s.reshape(bs, 1, AB_WIDTH), bnew=bnew,
                  name="fox_decode")
    o_fox_s = _head_diag(acc, FOX_HEADS)
    y_p, y_s = sublayers(y_p, y_s,
                         [([o_ret_p], w0a), ([o_fox_p], w0b)],
                         [([o_ret_s], w0a), ([o_fox_s], w0b)], 0)

    n_qkv = NSA_WIDTH + 6 * NSA_KV_WIDTH
    w1_main = _bf(w_in1[:, :n_qkv])
    w1_g = _bf(_pad_cols(w_in1[:, n_qkv:]))
    zero_b = jnp.zeros((1, LANE), F32)
    w_o1 = _bf(w_out1)
    consts = _compress_consts(a_ck, a_cv, w_ck, w_cv)
    kv_split = lambda hh, i: hh[:, NSA_WIDTH + i * NSA_KV_WIDTH:NSA_WIDTH + (i + 1) * NSA_KV_WIDTH]
    h1 = _matmul(y_p, w1_main, 1024, 256, "nsa_proj_prompt")
    gates_p = _gate(y_p, w1_g, zero_b, "sigmoid", "nsa_gate_prompt")[0]
    nb = t // CMP_BLOCK
    nbp = -(-nb // LANE) * LANE
    kc_c, vc_c = _compress_prompt(h1, t, consts)
    if nbp > nb:
        kc_c = jnp.pad(kc_c, ((0, nbp - nb), (0, 0)))
        vc_c = jnp.pad(vc_c, ((0, nbp - nb), (0, 0)))
    o_cmp_p, sel_p = _cmp_select_prompt(h1, kc_c, vc_c, gates_p, t, nbp)
    o_sel_p = _sel_flash_prompt(h1, sel_p, gates_p, t, nbp)
    o_win_p = _win_prompt(h1, gates_p, t)
    kv4 = lambda a, n: a.reshape(-1, n, NSA_KV_HEADS, NSA_HEAD_DIM)
    nsa_p = [kv4(kv_split(h1, i), t) for i in range(4)]
    nsa_p += [kv4(kv_split(h1, i)[t - win_buf:], win_buf) for i in (4, 5)]
    h1_s = _matmul(y_s, w1_main, bs, 256, "nsa_proj_step")
    gates_s = _gate(y_s, w1_g, zero_b, "sigmoid", "nsa_gate_step")[0]
    qrows = _nsa_rows(h1_s[:, :NSA_WIDTH])
    pool3 = lambda a: a.reshape(a.shape[0], PAGE_SIZE, NSA_KV_WIDTH)
    kc_s, vc_s = _compress_paged(pool3(cache_cmp_k), pool3(cache_cmp_v), page_table, consts)
    nbc = past // CMP_BLOCK
    nbp_s = -(-nbc // LANE) * LANE
    if nbp_s > nbc:
        kc_s = jnp.pad(kc_s, ((0, 0), (0, nbp_s - nbc), (0, 0)))
        vc_s = jnp.pad(vc_s, ((0, 0), (0, nbp_s - nbc), (0, 0)))
    acc_cmp, sel_s = _cmp_select_step(qrows, kc_s, vc_s, _gate_rows(gates_s, 0), nbc)
    scale = NSA_HEAD_DIM ** -0.5
    new_kv = lambda i: kv_split(h1_s, i).reshape(bs, 1, NSA_KV_WIDTH)
    pos_minor = lambda a: jnp.transpose(a, (0, 2, 3, 1))
    acc_sel = _decode(qrows, pos_minor(cache_slc_k), pos_minor(cache_slc_v), page_table, 8, scale,
                      sel=sel_s, knew=new_kv(2), vnew=new_kv(3), rowscale=_gate_rows(gates_s, 1),
                      tpages=1, name="nsa_sel_decode")
    wpages = win_buf // PAGE_SIZE
    acc_win = _decode(qrows, pos_minor(cache_win_k), pos_minor(cache_win_v),
                      _contig_table(bs, wpages), wpages, scale,
                      knew=new_kv(4), vnew=new_kv(5), rowscale=_gate_rows(gates_s, 2),
                      tpages=wpages, name="nsa_win_decode")
    nsa_s = [kv4(kv_split(h1_s, i), 1) for i in range(4)]
    nsa_s += [jnp.concatenate([c, kv4(kv_split(h1_s, i), 1)], 1)[:, -win_buf:]
              for c, i in ((cache_win_k, 4), (cache_win_v, 5))]
    y_p, y_s = sublayers(y_p, y_s,
                         [([o_cmp_p, o_sel_p, o_win_p], w_o1)],
                         [([_nsa_diag(acc_cmp), _nsa_diag(acc_sel), _nsa_diag(acc_win)], w_o1)], 1)

    return (y_p.reshape(1, t, dm), y_s.reshape(bs, 1, dm), ret_p[None], ret_s,
            fk_p, fv_p, ff_p,
            fk_s.reshape(bs, 1, FOX_HEADS, AB_HEAD_DIM), fv_s.reshape(bs, 1, FOX_HEADS, AB_HEAD_DIM),
            logf_s[:, :FOX_HEADS].reshape(bs, 1, FOX_HEADS),
            *nsa_p, *nsa_s,
            jnp.stack(mem_k_list), jnp.stack(mem_v_list))
```
